```python
import jax, jax.numpy as jnp
from jax import lax
import numpy as np

D_MODEL = 2048
BATCH = 4
SEQ = 4096
DEPTH = 2

CHUNK = 64
Q_BLOCK = 128
N_MIXERS = 2
N_MLA_LAYERS = (DEPTH + 1) // 2
N_FOX_LAYERS = DEPTH // 2

MLA_HEADS = 16
MLA_NOPE_DIM = 128
MLA_ROPE_DIM = 64
MLA_V_DIM = 128
MLA_Q_RANK = 512
MLA_KV_RANK = 512
MLA_IN_DIM = MLA_Q_RANK + MLA_KV_RANK + MLA_ROPE_DIM
ROPE_THETA = 10000.0

FOX_HEADS = 16
FOX_HEAD_DIM = 128
FOX_WIDTH = FOX_HEADS * FOX_HEAD_DIM
FOX_IN_DIM = 3 * FOX_WIDTH + FOX_HEADS

FFN_HIDDEN = -(-(8 * D_MODEL) // (3 * 256)) * 256

DEEPNORM_ALPHA = float((2 * DEPTH) ** 0.25)
DEEPNORM_BETA = float((8 * DEPTH) ** -0.25)
LN_EPS = 1e-5
RMS_EPS = 1e-6
NEG_INF = -1e30

kernel_name = "mla_fox_interleaved_deepnorm_trunk"


def layer_norm(x, g, b):
    xf = x.astype(jnp.float32)
    mu = jnp.mean(xf, axis=-1, keepdims=True)
    var = jnp.mean(jnp.square(xf - mu), axis=-1, keepdims=True)
    y = (xf - mu) * lax.rsqrt(var + LN_EPS)
    return (y * g.astype(jnp.float32) + b.astype(jnp.float32)).astype(x.dtype)


def rms_norm(x, g):
    xf = x.astype(jnp.float32)
    y = xf * lax.rsqrt(jnp.mean(jnp.square(xf), axis=-1, keepdims=True) + RMS_EPS)
    return (y * g.astype(jnp.float32)).astype(x.dtype)


def rope_tables(positions, dim):
    inv_freq = ROPE_THETA ** (-jnp.arange(0, dim, 2, dtype=jnp.float32) / dim)
    ang = positions.astype(jnp.float32)[..., None] * inv_freq
    return jnp.cos(ang), jnp.sin(ang)


def apply_rope(t, cos, sin):
    half = t.shape[-1] // 2
    t1 = t[..., :half].astype(jnp.float32)
    t2 = t[..., half:].astype(jnp.float32)
    out = jnp.concatenate([t1 * cos - t2 * sin, t2 * cos + t1 * sin], axis=-1)
    return out.astype(t.dtype)


def to_blocks(t):
    b, s = t.shape[0], t.shape[1]
    return jnp.moveaxis(t.reshape(b, s // Q_BLOCK, Q_BLOCK, *t.shape[2:]), 1, 0)


def from_blocks(t):
    n, b = t.shape[0], t.shape[1]
    return jnp.moveaxis(t, 0, 1).reshape(b, n * Q_BLOCK, *t.shape[3:])


def mla_mixer(x, cos, sin, w_in, q_norm_g, w_q_up, kv_norm_g, w_kv_up, w_o):
    b, s, _ = x.shape
    h = jnp.einsum('bsd,de->bse', x, w_in)
    c_q = h[..., :MLA_Q_RANK]
    c_kv = h[..., MLA_Q_RANK:MLA_Q_RANK + MLA_KV_RANK]
    k_rope = h[..., MLA_Q_RANK + MLA_KV_RANK:]
    q = jnp.einsum('bsr,re->bse', rms_norm(c_q, q_norm_g), w_q_up)
    q = q.reshape(b, s, MLA_HEADS, MLA_NOPE_DIM + MLA_ROPE_DIM)
    q_nope = q[..., :MLA_NOPE_DIM]
    q_rope = apply_rope(q[..., MLA_NOPE_DIM:], cos[:, :, None, :], sin[:, :, None, :])
    k_rope = apply_rope(k_rope, cos, sin)
    kv = jnp.einsum('bsr,re->bse', rms_norm(c_kv, kv_norm_g), w_kv_up)
    kv = kv.reshape(b, s, MLA_HEADS, MLA_NOPE_DIM + MLA_V_DIM)
    k_nope = kv[..., :MLA_NOPE_DIM]
    v = kv[..., MLA_NOPE_DIM:]
    scale = (MLA_NOPE_DIM + MLA_ROPE_DIM) ** -0.5
    k_chunk = jnp.arange(s) // CHUNK

    def block(args):
        i, qn, qr = args
        sc = (jnp.einsum('bqhn,bkhn->bhqk', qn, k_nope)
              + jnp.einsum('bqhr,bkr->bhqk', qr, k_rope)).astype(jnp.float32) * scale
        q_chunk = (i * Q_BLOCK + jnp.arange(Q_BLOCK)) // CHUNK
        allowed = k_chunk[None, :] <= q_chunk[:, None]
        sc = jnp.where(allowed, sc, NEG_INF)
        p = jax.nn.softmax(sc, axis=-1).astype(v.dtype)
        return jnp.einsum('bhqk,bkhv->bqhv', p, v)

    n_blk = s // Q_BLOCK
    o = lax.map(block, (jnp.arange(n_blk), to_blocks(q_nope), to_blocks(q_rope)))
    o = from_blocks(o).reshape(b, s, MLA_HEADS * MLA_V_DIM)
    return jnp.einsum('bse,ed->bsd', o, w_o)


def fox_mixer(x, w_in, b_f, w_o):
    b, s, _ = x.shape
    h = jnp.einsum('bsd,de->bse', x, w_in)
    q = h[..., :FOX_WIDTH].reshape(b, s, FOX_HEADS, FOX_HEAD_DIM)
    k = h[..., FOX_WIDTH:2 * FOX_WIDTH].reshape(b, s, FOX_HEADS, FOX_HEAD_DIM)
    v = h[..., 2 * FOX_WIDTH:3 * FOX_WIDTH].reshape(b, s, FOX_HEADS, FOX_HEAD_DIM)
    f_logit = h[..., 3 * FOX_WIDTH:].astype(jnp.float32) + b_f.astype(jnp.float32)
    log_f = jax.nn.log_sigmoid(f_logit)
    c = jnp.cumsum(log_f, axis=1)
    c_k = jnp.transpose(c, (0, 2, 1))
    scale = FOX_HEAD_DIM ** -0.5
    k_pos = jnp.arange(s)

    def block(args):
        i, qb, cb = args
        sc = jnp.einsum('bqhd,bkhd->bhqk', qb, k).astype(jnp.float32) * scale
        sc = sc + jnp.transpose(cb, (0, 2, 1))[..., None] - c_k[:, :, None, :]
        q_pos = i * Q_BLOCK + jnp.arange(Q_BLOCK)
        sc = jnp.where(k_pos[None, :] <= q_pos[:, None], sc, NEG_INF)
        p = jax.nn.softmax(sc, axis=-1).astype(v.dtype)
        return jnp.einsum('bhqk,bkhd->bqhd', p, v)

    n_blk = s // Q_BLOCK
    o = lax.map(block, (jnp.arange(n_blk), to_blocks(q), to_blocks(c)))
    o = from_blocks(o).reshape(b, s, FOX_WIDTH)
    return jnp.einsum('bse,ed->bsd', o, w_o)


def swiglu_ffn(x, w_gu, w_down):
    gu = jnp.einsum('bsd,df->bsf', x, w_gu)
    g, u = gu[..., :FFN_HIDDEN], gu[..., FFN_HIDDEN:]
    return jnp.einsum('bsf,fd->bsd', jax.nn.silu(g) * u, w_down)


def _dense(key, shape, fan_in, scale=1.0):
    return jax.random.normal(key, shape, jnp.float32) * (scale * fan_in ** -0.5)


def setup_inputs(seed: int = 0) -> dict:
    key = jax.random.key(seed)
    ks = jax.random.split(key, 24)
    NA, NF, L, D = N_MLA_LAYERS, N_FOX_LAYERS, DEPTH, D_MODEL
    beta = DEEPNORM_BETA
    x = jax.random.normal(ks[0], (BATCH, SEQ, D), jnp.float32)
    offset = jax.random.randint(ks[1], (BATCH, 1), 0, 1024, dtype=jnp.int32) * CHUNK
    positions = (offset + jnp.arange(SEQ, dtype=jnp.int32)[None, :]).astype(jnp.int32)

    mla_w_in = _dense(ks[2], (NA, D, MLA_IN_DIM), D)
    mla_q_norm_g = 1.0 + 0.02 * jax.random.normal(ks[3], (NA, MLA_Q_RANK), jnp.float32)
    mla_w_q_up = _dense(ks[4], (NA, MLA_Q_RANK, MLA_HEADS * (MLA_NOPE_DIM + MLA_ROPE_DIM)), MLA_Q_RANK)
    mla_kv_norm_g = 1.0 + 0.02 * jax.random.normal(ks[5], (NA, MLA_KV_RANK), jnp.float32)
    wk = _dense(ks[6], (NA, MLA_KV_RANK, MLA_HEADS, MLA_NOPE_DIM), MLA_KV_RANK)
    wv = _dense(ks[7], (NA, MLA_KV_RANK, MLA_HEADS, MLA_V_DIM), MLA_KV_RANK, beta)
    mla_w_kv_up = jnp.concatenate([wk, wv], axis=-1).reshape(
        NA, MLA_KV_RANK, MLA_HEADS * (MLA_NOPE_DIM + MLA_V_DIM))
    mla_w_o = _dense(ks[8], (NA, MLA_HEADS * MLA_V_DIM, D), MLA_HEADS * MLA_V_DIM, beta)

    fq = _dense(ks[9], (NF, D, FOX_WIDTH), D)
    fk = _dense(ks[10], (NF, D, FOX_WIDTH), D)
    fv = _dense(ks[11], (NF, D, FOX_WIDTH), D, beta)
    ff = _dense(ks[12], (NF, D, FOX_HEADS), D, 0.1)
    fox_w_in = jnp.concatenate([fq, fk, fv, ff], axis=-1)
    fox_b_f = 4.0 + 0.5 * jax.random.normal(ks[13], (NF, FOX_HEADS), jnp.float32)
    fox_w_o = _dense(ks[14], (NF, FOX_WIDTH, D), FOX_WIDTH, beta)

    ffn_w_gu = _dense(ks[15], (L, D, 2 * FFN_HIDDEN), D, beta)
    ffn_w_down = _dense(ks[16], (L, FFN_HIDDEN, D), FFN_HIDDEN, beta)
    ln_mix_g = 1.0 + 0.02 * jax.random.normal(ks[17], (L, D), jnp.float32)
    ln_mix_b = 0.02 * jax.random.normal(ks[18], (L, D), jnp.float32)
    ln_ffn_g = 1.0 + 0.02 * jax.random.normal(ks[19], (L, D), jnp.float32)
    ln_ffn_b = 0.02 * jax.random.normal(ks[20], (L, D), jnp.float32)
    return {
        "x": x, "positions": positions,
        "mla_w_in": mla_w_in, "mla_q_norm_g": mla_q_norm_g, "mla_w_q_up": mla_w_q_up,
        "mla_kv_norm_g": mla_kv_norm_g, "mla_w_kv_up": mla_w_kv_up, "mla_w_o": mla_w_o,
        "fox_w_in": fox_w_in, "fox_b_f": fox_b_f, "fox_w_o": fox_w_o,
        "ffn_w_gu": ffn_w_gu, "ffn_w_down": ffn_w_down,
        "ln_mix_g": ln_mix_g, "ln_mix_b": ln_mix_b, "ln_ffn_g": ln_ffn_g, "ln_ffn_b": ln_ffn_b,
    }


def reference(x, positions, mla_w_in, mla_q_norm_g, mla_w_q_up, mla_kv_norm_g, mla_w_kv_up,
              mla_w_o, fox_w_in, fox_b_f, fox_w_o, ffn_w_gu, ffn_w_down,
              ln_mix_g, ln_mix_b, ln_ffn_g, ln_ffn_b):
    cos, sin = rope_tables(positions, MLA_ROPE_DIM)
    for i in range(DEPTH):
        j = i // N_MIXERS
        if i % N_MIXERS == 0:
            mix = mla_mixer(x, cos, sin, mla_w_in[j], mla_q_norm_g[j], mla_w_q_up[j],
                            mla_kv_norm_g[j], mla_w_kv_up[j], mla_w_o[j])
        else:
            mix = fox_mixer(x, fox_w_in[j], fox_b_f[j], fox_w_o[j])
        x = layer_norm(DEEPNORM_ALPHA * x + mix, ln_mix_g[i], ln_mix_b[i])
        x = layer_norm(DEEPNORM_ALPHA * x + swiglu_ffn(x, ffn_w_gu[i], ffn_w_down[i]),
                       ln_ffn_g[i], ln_ffn_b[i])
    return x
```

```python
import functools

import jax
import jax.numpy as jnp
import numpy as np
from jax import lax
from jax.experimental import pallas as pl
from jax.experimental.pallas import tpu as pltpu

CHUNK = 64
MLA_HEADS = 16
MLA_NOPE_DIM = 128
MLA_ROPE_DIM = 64
MLA_V_DIM = 128
MLA_Q_RANK = 512
MLA_KV_RANK = 512
ROPE_THETA = 10000.0
FOX_HEADS = 16
FOX_HEAD_DIM = 128
LN_EPS = 1e-5
RMS_EPS = 1e-6
NEG_INF = -1e30

LANES = 128
VMEM_LIMIT_BYTES = 56 * 1024 * 1024

BF16 = jnp.bfloat16
F32 = jnp.float32


def _params(*semantics):
    return pltpu.CompilerParams(dimension_semantics=semantics, vmem_limit_bytes=VMEM_LIMIT_BYTES)


def _dot(a, b):
    return jnp.dot(a, b, preferred_element_type=F32)


def _layer_norm(y, g, b):
    mu = jnp.mean(y, axis=-1, keepdims=True)
    d = y - mu
    var = jnp.mean(d * d, axis=-1, keepdims=True)
    return d * lax.rsqrt(var + LN_EPS) * g + b


def _rms_norm(y, g):
    return y * lax.rsqrt(jnp.mean(y * y, axis=-1, keepdims=True) + RMS_EPS) * g


def _mla_in_kernel(x_ref, pos_ref, freq_ref, w_ref, gq_ref, gkv_ref,
                   cq_ref, ckv_ref, kr_ref, cos_ref, sin_ref):
    h = _dot(x_ref[...].astype(BF16), w_ref[...])
    cq_ref[...] = _rms_norm(h[:, :MLA_Q_RANK], gq_ref[...]).astype(BF16)
    ckv_ref[...] = _rms_norm(h[:, MLA_Q_RANK:MLA_Q_RANK + MLA_KV_RANK], gkv_ref[...]).astype(BF16)
    ang = pos_ref[...] * freq_ref[...]
    c = jnp.cos(ang)
    s = jnp.sin(ang)
    base = MLA_Q_RANK + MLA_KV_RANK
    kr_ref[...] = (h[:, base:base + LANES] * c + h[:, base + LANES:] * s).astype(BF16)
    cos_ref[...] = c
    sin_ref[...] = s


def _mla_in(x, pos, freq, w, gq, gkv, tm=512):
    n, d = x.shape
    row = lambda i: (i, 0)
    fixed = lambda i: (0, 0)
    return pl.pallas_call(
        _mla_in_kernel,
        grid=(n // tm,),
        in_specs=[pl.BlockSpec((tm, d), row), pl.BlockSpec((tm, 1), row),
                  pl.BlockSpec((1, LANES), fixed), pl.BlockSpec(w.shape, fixed),
                  pl.BlockSpec((1, MLA_Q_RANK), fixed), pl.BlockSpec((1, MLA_KV_RANK), fixed)],
        out_specs=[pl.BlockSpec((tm, MLA_Q_RANK), row), pl.BlockSpec((tm, MLA_KV_RANK), row),
                   pl.BlockSpec((tm, LANES), row), pl.BlockSpec((tm, LANES), row),
                   pl.BlockSpec((tm, LANES), row)],
        out_shape=[jax.ShapeDtypeStruct((n, MLA_Q_RANK), BF16),
                   jax.ShapeDtypeStruct((n, MLA_KV_RANK), BF16),
                   jax.ShapeDtypeStruct((n, LANES), BF16),
                   jax.ShapeDtypeStruct((n, LANES), F32),
                   jax.ShapeDtypeStruct((n, LANES), F32)],
        compiler_params=_params("arbitrary"),
        name="mla_in",
    )(x, pos, freq, w, gq, gkv)


def _mla_q_kernel(cq_ref, cos_ref, sin_ref, w_ref, qn_ref, qr_ref, *, scale):
    n_nope = MLA_HEADS * MLA_NOPE_DIM
    n_rope = MLA_HEADS * MLA_ROPE_DIM
    cq = cq_ref[...]
    qn_ref[...] = (_dot(cq, w_ref[:, :n_nope]) * scale).astype(BF16)
    r = _dot(cq, w_ref[:, n_nope:n_nope + n_rope])
    rp = _dot(cq, w_ref[:, n_nope + n_rope:])
    c = cos_ref[...]
    s = sin_ref[...]
    for p in range(n_rope // LANES):
        sl = slice(p * LANES, (p + 1) * LANES)
        qr_ref[:, sl] = ((r[:, sl] * c + rp[:, sl] * s) * scale).astype(BF16)


def _mla_q(cq, cos, sin, w, scale, tm=512):
    n = cq.shape[0]
    n_nope = MLA_HEADS * MLA_NOPE_DIM
    n_rope = MLA_HEADS * MLA_ROPE_DIM
    row = lambda i: (i, 0)
    fixed = lambda i: (0, 0)
    return pl.pallas_call(
        functools.partial(_mla_q_kernel, scale=scale),
        grid=(n // tm,),
        in_specs=[pl.BlockSpec((tm, MLA_Q_RANK), row), pl.BlockSpec((tm, LANES), row),
                  pl.BlockSpec((tm, LANES), row), pl.BlockSpec(w.shape, fixed)],
        out_specs=[pl.BlockSpec((tm, n_nope), row), pl.BlockSpec((tm, n_rope), row)],
        out_shape=[jax.ShapeDtypeStruct((n, n_nope), BF16), jax.ShapeDtypeStruct((n, n_rope), BF16)],
        compiler_params=_params("arbitrary"),
        name="mla_q_up",
    )(cq, cos, sin, w)


def _matmul_kernel(a_ref, b_ref, o_ref, a_bf, *, n_scaled, scale):
    j = pl.program_id(1)

    @pl.when(j == 0)
    def _():
        a_bf[...] = a_ref[...].astype(BF16)

    acc = _dot(a_bf[...], b_ref[...])
    if n_scaled:
        acc = acc * jnp.where(j < n_scaled, scale, 1.0).astype(F32)
    o_ref[...] = acc.astype(o_ref.dtype)


def _matmul(a, b, tm, tn, n_scaled=0, scale=1.0):
    m, k = a.shape
    n = b.shape[1]
    return pl.pallas_call(
        functools.partial(_matmul_kernel, n_scaled=n_scaled, scale=scale),
        grid=(m // tm, n // tn),
        in_specs=[pl.BlockSpec((tm, k), lambda i, j: (i, 0)), pl.BlockSpec((k, tn), lambda i, j: (0, j))],
        out_specs=pl.BlockSpec((tm, tn), lambda i, j: (i, j)),
        out_shape=jax.ShapeDtypeStruct((m, n), BF16),
        scratch_shapes=[pltpu.VMEM((tm, k), BF16)],
        compiler_params=_params("arbitrary", "arbitrary"),
        name="matmul",
    )(a, b)


def _split3(v):
    hi = v.astype(BF16).astype(F32)
    r1 = v - hi
    mid = r1.astype(BF16).astype(F32)
    lo = (r1 - mid).astype(BF16).astype(F32)
    return hi, mid, lo


def _fox_gate_kernel(x_ref, w_ref, b_ref, tri_ref, aq_ref, ak_ref, carry):
    t = pl.program_id(1)

    @pl.when(t == 0)
    def _():
        carry[...] = jnp.zeros_like(carry)

    logit = _dot(x_ref[...].astype(BF16), w_ref[...]) + b_ref[...]
    log_f = jnp.minimum(logit, 0.0) - jnp.log1p(jnp.exp(-jnp.abs(logit)))
    tri = tri_ref[...]
    hi, mid, lo = _split3(log_f)
    c = (_dot(tri, hi.astype(BF16)) + _dot(tri, mid.astype(BF16))) + _dot(tri, lo.astype(BF16))
    c = c + carry[0:1, :]
    ts = c.shape[0]
    carry[0:1, :] = c[ts - 1:ts, :]
    hi, mid, lo = _split3(c)
    group = lax.broadcasted_iota(jnp.int32, c.shape, 1) // FOX_HEADS
    pieces = jnp.where(group % 3 == 0, hi, jnp.where(group % 3 == 1, mid, lo))
    aq_ref[...] = jnp.where(group < 3, pieces, jnp.where(group < 6, 1.0, 0.0)).astype(BF16)
    ak_ref[...] = jnp.where(group < 3, 1.0, jnp.where(group < 6, -pieces, 0.0)).astype(BF16)


def _fox_gate(x, w, b, batch, ts=512):
    n, d = x.shape
    steps = n // batch // ts
    tri = jnp.asarray(np.tril(np.ones((ts, ts), np.float32)), BF16)
    row = lambda bi, t: (bi * steps + t, 0)
    fixed = lambda bi, t: (0, 0)
    return pl.pallas_call(
        _fox_gate_kernel,
        grid=(batch, steps),
        in_specs=[pl.BlockSpec((ts, d), row), pl.BlockSpec(w.shape, fixed),
                  pl.BlockSpec((1, LANES), fixed), pl.BlockSpec((ts, ts), fixed)],
        out_specs=[pl.BlockSpec((ts, LANES), row), pl.BlockSpec((ts, LANES), row)],
        out_shape=[jax.ShapeDtypeStruct((n, LANES), BF16), jax.ShapeDtypeStruct((n, LANES), BF16)],
        scratch_shapes=[pltpu.VMEM((8, LANES), F32)],
        compiler_params=_params("arbitrary", "arbitrary"),
        name="fox_gate",
    )(x, w, b, tri)


def _attn_kernel(qm_ref, qa_ref, lm_ref, km_ref, ka_ref, v_ref, bias_ref, o_ref,
                 kcat, vcat, m_sc, acc_sc, *, tq):
    i = pl.program_id(2)

    @pl.when(i == 0)
    def _():
        kcat[:, :LANES] = km_ref[...]
        kcat[:, LANES:] = ka_ref[...]
        vcat[:, :LANES] = v_ref[...]
        lane = lax.broadcasted_iota(jnp.int32, v_ref.shape, 1)
        vcat[:, LANES:] = jnp.where(lane == 0, 1.0, 0.0).astype(BF16)

    q = jnp.concatenate([qm_ref[...], qa_ref[...] * lm_ref[0, 0:1, :]], axis=1)

    def scores(j):
        k = kcat[pl.ds(pl.multiple_of(j * tq, tq), tq), :]
        return lax.dot_general(q, k, (((1,), (1,)), ((), ())), preferred_element_type=F32)

    def pv(p, j):
        return _dot(p.astype(BF16), vcat[pl.ds(pl.multiple_of(j * tq, tq), tq), :])

    s = scores(i) + bias_ref[...]
    m0 = jnp.max(s, axis=-1, keepdims=True)
    m_sc[...] = m0
    acc_sc[...] = pv(jnp.exp(s - m0), i)

    def body(j, carry):
        s = scores(j)
        m_prev = m_sc[...]
        m_new = jnp.maximum(m_prev, jnp.max(s, axis=-1, keepdims=True))
        acc_sc[...] = jnp.exp(m_prev - m_new) * acc_sc[...] + pv(jnp.exp(s - m_new), j)
        m_sc[...] = m_new
        return carry

    lax.fori_loop(0, i, body, 0)
    acc = acc_sc[...]
    o_ref[...] = (acc[:, :LANES] / acc[:, LANES:LANES + 1]).astype(o_ref.dtype)


def _attention(qm, qm_col0, qa, qa_col, lane_mask, km, km_col0, ka, v, v_col0, bias, batch, heads, tq=512):
    n = qm.shape[0]
    seq = n // batch
    nq = seq // tq
    q_row = lambda b, h, i: b * nq + i
    return pl.pallas_call(
        functools.partial(_attn_kernel, tq=tq),
        grid=(batch, heads, nq),
        in_specs=[
            pl.BlockSpec((tq, LANES), lambda b, h, i: (q_row(b, h, i), qm_col0 + h)),
            pl.BlockSpec((tq, LANES), lambda b, h, i: (q_row(b, h, i), qa_col(h))),
            pl.BlockSpec((1, 8, LANES), lambda b, h, i: (h, 0, 0)),
            pl.BlockSpec((seq, LANES), lambda b, h, i: (b, km_col0 + h)),
            pl.BlockSpec((seq, LANES), lambda b, h, i: (b, 0)),
            pl.BlockSpec((seq, LANES), lambda b, h, i: (b, v_col0 + h)),
            pl.BlockSpec((tq, tq), lambda b, h, i: (0, 0)),
        ],
        out_specs=pl.BlockSpec((tq, LANES), lambda b, h, i: (q_row(b, h, i), h)),
        out_shape=jax.ShapeDtypeStruct((n, heads * LANES), BF16),
        scratch_shapes=[pltpu.VMEM((seq, 2 * LANES), BF16), pltpu.VMEM((seq, 2 * LANES), BF16),
                        pltpu.VMEM((tq, 1), F32), pltpu.VMEM((tq, 2 * LANES), F32)],
        compiler_params=_params("arbitrary", "arbitrary", "arbitrary"),
        name="attention",
    )(qm, qa, lane_mask, km, ka, v, bias)


def _proj_ln_kernel(o_ref, w_ref, x_ref, g_ref, b_ref, y_ref, *, alpha):
    y = alpha * x_ref[...] + _dot(o_ref[...], w_ref[...])
    y_ref[...] = _layer_norm(y, g_ref[...], b_ref[...])


def _proj_ln(o, w, x, g, b, alpha, tm=512):
    n, d = x.shape
    row = lambda i: (i, 0)
    fixed = lambda i: (0, 0)
    return pl.pallas_call(
        functools.partial(_proj_ln_kernel, alpha=alpha),
        grid=(n // tm,),
        in_specs=[pl.BlockSpec((tm, o.shape[1]), row), pl.BlockSpec(w.shape, fixed),
                  pl.BlockSpec((tm, d), row), pl.BlockSpec((1, d), fixed), pl.BlockSpec((1, d), fixed)],
        out_specs=pl.BlockSpec((tm, d), row),
        out_shape=jax.ShapeDtypeStruct((n, d), F32),
        compiler_params=_params("arbitrary"),
        name="proj_ln",
    )(o, w, x, g, b)


def _ffn_kernel(x_ref, wg_ref, wu_ref, wd_ref, g_ref, b_ref, y_ref, x_bf, acc, *, alpha):
    j = pl.program_id(1)

    @pl.when(j == 0)
    def _():
        x_bf[...] = x_ref[...].astype(BF16)

    xb = x_bf[...]
    gate = _dot(xb, wg_ref[...])
    up = _dot(xb, wu_ref[...])
    part = _dot((gate * jax.nn.sigmoid(gate) * up).astype(BF16), wd_ref[...])

    @pl.when(j == 0)
    def _():
        acc[...] = part

    @pl.when(j > 0)
    def _():
        acc[...] += part

    @pl.when(j == pl.num_programs(1) - 1)
    def _():
        y_ref[...] = _layer_norm(alpha * x_ref[...] + acc[...], g_ref[...], b_ref[...])


def _ffn(x, w_gu, w_down, g, b, alpha, tm=512, tf=512):
    n, d = x.shape
    hidden = w_down.shape[0]
    nf = hidden // tf
    return pl.pallas_call(
        functools.partial(_ffn_kernel, alpha=alpha),
        grid=(n // tm, nf),
        in_specs=[pl.BlockSpec((tm, d), lambda i, j: (i, 0)),
                  pl.BlockSpec((d, tf), lambda i, j: (0, j)),
                  pl.BlockSpec((d, tf), lambda i, j: (0, nf + j)),
                  pl.BlockSpec((tf, d), lambda i, j: (j, 0)),
                  pl.BlockSpec((1, d), lambda i, j: (0, 0)), pl.BlockSpec((1, d), lambda i, j: (0, 0))],
        out_specs=pl.BlockSpec((tm, d), lambda i, j: (i, 0)),
        out_shape=jax.ShapeDtypeStruct((n, d), F32),
        scratch_shapes=[pltpu.VMEM((tm, d), BF16), pltpu.VMEM((tm, d), F32)],
        compiler_params=_params("arbitrary", "arbitrary"),
        name="ffn",
    )(x, w_gu, w_gu, w_down, g, b)


def _mla_weights(w_in, w_q_up, w_kv_up):
    d = w_in.shape[0]
    half = MLA_ROPE_DIM // 2
    base = MLA_Q_RANK + MLA_KV_RANK
    k1, k2 = w_in[:, base:base + half], w_in[:, base + half:]
    w_in_cat = jnp.concatenate([w_in[:, :base], k1, k2, k1, k2, -k2, k1, -k2, k1], axis=1)
    wq = w_q_up.reshape(MLA_Q_RANK, MLA_HEADS, MLA_NOPE_DIM + MLA_ROPE_DIM)
    r1 = wq[:, :, MLA_NOPE_DIM:MLA_NOPE_DIM + half]
    r2 = wq[:, :, MLA_NOPE_DIM + half:]
    w_q_cat = jnp.concatenate([
        wq[:, :, :MLA_NOPE_DIM].reshape(MLA_Q_RANK, -1),
        wq[:, :, MLA_NOPE_DIM:].reshape(MLA_Q_RANK, -1),
        jnp.concatenate([-r2, r1], axis=2).reshape(MLA_Q_RANK, -1)], axis=1)
    wkv = w_kv_up.reshape(MLA_KV_RANK, MLA_HEADS, MLA_NOPE_DIM + MLA_V_DIM)
    w_kv_cat = jnp.concatenate([wkv[:, :, :MLA_NOPE_DIM].reshape(MLA_KV_RANK, -1),
                                wkv[:, :, MLA_NOPE_DIM:].reshape(MLA_KV_RANK, -1)], axis=1)
    del d
    return w_in_cat.astype(BF16), w_q_cat.astype(BF16), w_kv_cat.astype(BF16)


def _causal_bias(tq, granularity):
    r = np.arange(tq)[:, None] // granularity
    c = np.arange(tq)[None, :] // granularity
    return jnp.asarray(np.where(c <= r, 0.0, NEG_INF), F32)


def _mla_lane_mask():
    lane = np.arange(LANES)[None, :]
    head = np.arange(MLA_HEADS)[:, None]
    m = (lane // MLA_ROPE_DIM == head % 2).astype(np.float32)
    return jnp.asarray(np.broadcast_to(m[:, None, :], (MLA_HEADS, 8, LANES)), BF16)


def _fox_lane_mask():
    lane = np.arange(LANES)[None, :]
    head = np.arange(FOX_HEADS)[:, None]
    m = ((lane % FOX_HEADS == head) & (lane < 6 * FOX_HEADS)).astype(np.float32)
    return jnp.asarray(np.broadcast_to(m[:, None, :], (FOX_HEADS, 8, LANES)), BF16)


def _mla_layer(x, pos, w_in, q_norm_g, w_q_up, kv_norm_g, w_kv_up, w_o, batch):
    w_in_cat, w_q_cat, w_kv_cat = _mla_weights(w_in, w_q_up, w_kv_up)
    half = MLA_ROPE_DIM // 2
    inv_freq = ROPE_THETA ** (-jnp.arange(0, MLA_ROPE_DIM, 2, dtype=F32) / MLA_ROPE_DIM)
    freq = jnp.tile(inv_freq, LANES // half)[None, :]
    cq, ckv, kr, cos, sin = _mla_in(x, pos, freq, w_in_cat, q_norm_g[None, :], kv_norm_g[None, :])
    scale = (MLA_NOPE_DIM + MLA_ROPE_DIM) ** -0.5
    qn, qr = _mla_q(cq, cos, sin, w_q_cat, scale)
    kv = _matmul(ckv, w_kv_cat, tm=1024, tn=1024)
    o = _attention(qn, 0, qr, lambda h: h // 2, _mla_lane_mask(), kv, 0, kr, kv, MLA_HEADS,
                   _causal_bias(512, CHUNK), batch, MLA_HEADS)
    return o, w_o.astype(BF16)


def _fox_layer(x, w_in, b_f, w_o, batch):
    width = FOX_HEADS * FOX_HEAD_DIM
    qkv = _matmul(x, w_in[:, :3 * width].astype(BF16), tm=1024, tn=1024,
                  n_scaled=width // 1024, scale=FOX_HEAD_DIM ** -0.5)
    groups = 6
    w_f = jnp.pad(jnp.tile(w_in[:, 3 * width:], (1, groups)), ((0, 0), (0, LANES - groups * FOX_HEADS)))
    b6 = jnp.pad(jnp.tile(b_f, groups), (0, LANES - groups * FOX_HEADS))[None, :]
    aq, ak = _fox_gate(x, w_f.astype(BF16), b6, batch)
    o = _attention(qkv, 0, aq, lambda h: 0, _fox_lane_mask(), qkv, FOX_HEADS, ak, qkv, 2 * FOX_HEADS,
                   _causal_bias(512, 1), batch, FOX_HEADS)
    return o, w_o.astype(BF16)


def kernel(x, positions, mla_w_in, mla_q_norm_g, mla_w_q_up, mla_kv_norm_g, mla_w_kv_up, mla_w_o,
           fox_w_in, fox_b_f, fox_w_o, ffn_w_gu, ffn_w_down, ln_mix_g, ln_mix_b, ln_ffn_g, ln_ffn_b):
    batch, seq, d = x.shape
    depth = ffn_w_gu.shape[0]
    alpha = float((2 * depth) ** 0.25)
    h = x.reshape(batch * seq, d)
    pos = positions.astype(F32).reshape(batch * seq, 1)
    for i in range(depth):
        j = i // 2
        if i % 2 == 0:
            o, w_o = _mla_layer(h, pos, mla_w_in[j], mla_q_norm_g[j], mla_w_q_up[j], mla_kv_norm_g[j],
                                mla_w_kv_up[j], mla_w_o[j], batch)
        else:
            o, w_o = _fox_layer(h, fox_w_in[j], fox_b_f[j], fox_w_o[j], batch)
        h = _proj_ln(o, w_o, h, ln_mix_g[i][None, :], ln_mix_b[i][None, :], alpha)
        h = _ffn(h, ffn_w_gu[i].astype(BF16), ffn_w_down[i].astype(BF16),
                 ln_ffn_g[i][None, :], ln_ffn_b[i][None, :], alpha)
    return h.reshape(batch, seq, d)
```

```python
import functools

import jax
import jax.numpy as jnp
import numpy as np
from jax import lax
from jax.experimental import pallas as pl
from jax.experimental.pallas import tpu as pltpu

CHUNK = 64
MLA_HEADS = 16
MLA_NOPE_DIM = 128
MLA_ROPE_DIM = 64
MLA_V_DIM = 128
MLA_Q_RANK = 512
MLA_KV_RANK = 512
ROPE_THETA = 10000.0
FOX_HEADS = 16
FOX_HEAD_DIM = 128
LN_EPS = 1e-5
RMS_EPS = 1e-6
NEG_INF = -1e30
LOG2E = 1.4426950408889634

LANES = 128
VMEM_LIMIT_BYTES = 56 * 1024 * 1024
ATTN_HEADS_PER_STEP = 4

BF16 = jnp.bfloat16
F32 = jnp.float32


def _params(*semantics):
    return pltpu.CompilerParams(dimension_semantics=semantics, vmem_limit_bytes=VMEM_LIMIT_BYTES)


def _dot(a, b):
    return jnp.dot(a, b, preferred_element_type=F32)


def _layer_norm(y, g, b):
    mu = jnp.mean(y, axis=-1, keepdims=True)
    d = y - mu
    var = jnp.mean(d * d, axis=-1, keepdims=True)
    return d * lax.rsqrt(var + LN_EPS) * g + b


def _rms_norm(y, g):
    return y * lax.rsqrt(jnp.mean(y * y, axis=-1, keepdims=True) + RMS_EPS) * g


def _mla_in_kernel(x_ref, pos_ref, freq_ref, w_ref, gq_ref, gkv_ref,
                   cq_ref, ckv_ref, kr_ref, cos_ref, sin_ref):
    h = _dot(x_ref[...].astype(BF16), w_ref[...])
    cq_ref[...] = _rms_norm(h[:, :MLA_Q_RANK], gq_ref[...]).astype(BF16)
    ckv_ref[...] = _rms_norm(h[:, MLA_Q_RANK:MLA_Q_RANK + MLA_KV_RANK], gkv_ref[...]).astype(BF16)
    ang = pos_ref[...] * freq_ref[...]
    c = jnp.cos(ang)
    s = jnp.sin(ang)
    base = MLA_Q_RANK + MLA_KV_RANK
    kr_ref[...] = (h[:, base:base + LANES] * c + h[:, base + LANES:] * s).astype(BF16)
    cos_ref[...] = c
    sin_ref[...] = s


def _mla_in(x, pos, freq, w, gq, gkv, tm=512):
    n, d = x.shape
    row = lambda i: (i, 0)
    fixed = lambda i: (0, 0)
    return pl.pallas_call(
        _mla_in_kernel,
        grid=(n // tm,),
        in_specs=[pl.BlockSpec((tm, d), row), pl.BlockSpec((tm, 1), row),
                  pl.BlockSpec((1, LANES), fixed), pl.BlockSpec(w.shape, fixed),
                  pl.BlockSpec((1, MLA_Q_RANK), fixed), pl.BlockSpec((1, MLA_KV_RANK), fixed)],
        out_specs=[pl.BlockSpec((tm, MLA_Q_RANK), row), pl.BlockSpec((tm, MLA_KV_RANK), row),
                   pl.BlockSpec((tm, LANES), row), pl.BlockSpec((tm, LANES), row),
                   pl.BlockSpec((tm, LANES), row)],
        out_shape=[jax.ShapeDtypeStruct((n, MLA_Q_RANK), BF16),
                   jax.ShapeDtypeStruct((n, MLA_KV_RANK), BF16),
                   jax.ShapeDtypeStruct((n, LANES), BF16),
                   jax.ShapeDtypeStruct((n, LANES), F32),
                   jax.ShapeDtypeStruct((n, LANES), F32)],
        compiler_params=_params("arbitrary"),
        name="mla_in",
    )(x, pos, freq, w, gq, gkv)


def _mla_q_kernel(cq_ref, cos_ref, sin_ref, w_ref, qn_ref, qr_ref, *, scale):
    n_nope = MLA_HEADS * MLA_NOPE_DIM
    n_rope = MLA_HEADS * MLA_ROPE_DIM
    cq = cq_ref[...]
    qn_ref[...] = (_dot(cq, w_ref[:, :n_nope]) * scale).astype(BF16)
    r = _dot(cq, w_ref[:, n_nope:n_nope + n_rope])
    rp = _dot(cq, w_ref[:, n_nope + n_rope:])
    c = cos_ref[...]
    s = sin_ref[...]
    for p in range(n_rope // LANES):
        sl = slice(p * LANES, (p + 1) * LANES)
        qr_ref[:, sl] = ((r[:, sl] * c + rp[:, sl] * s) * scale).astype(BF16)


def _mla_q(cq, cos, sin, w, scale, tm=512):
    n = cq.shape[0]
    n_nope = MLA_HEADS * MLA_NOPE_DIM
    n_rope = MLA_HEADS * MLA_ROPE_DIM
    row = lambda i: (i, 0)
    fixed = lambda i: (0, 0)
    return pl.pallas_call(
        functools.partial(_mla_q_kernel, scale=scale),
        grid=(n // tm,),
        in_specs=[pl.BlockSpec((tm, MLA_Q_RANK), row), pl.BlockSpec((tm, LANES), row),
                  pl.BlockSpec((tm, LANES), row), pl.BlockSpec(w.shape, fixed)],
        out_specs=[pl.BlockSpec((tm, n_nope), row), pl.BlockSpec((tm, n_rope), row)],
        out_shape=[jax.ShapeDtypeStruct((n, n_nope), BF16), jax.ShapeDtypeStruct((n, n_rope), BF16)],
        compiler_params=_params("arbitrary"),
        name="mla_q_up",
    )(cq, cos, sin, w)


def _matmul_kernel(a_ref, b_ref, o_ref, a_bf, *, n_scaled, scale):
    j = pl.program_id(1)

    @pl.when(j == 0)
    def _():
        a_bf[...] = a_ref[...].astype(BF16)

    acc = _dot(a_bf[...], b_ref[...])
    if n_scaled:
        acc = acc * jnp.where(j < n_scaled, scale, 1.0).astype(F32)
    o_ref[...] = acc.astype(o_ref.dtype)


def _matmul(a, b, tm, tn, n_scaled=0, scale=1.0):
    m, k = a.shape
    n = b.shape[1]
    return pl.pallas_call(
        functools.partial(_matmul_kernel, n_scaled=n_scaled, scale=scale),
        grid=(m // tm, n // tn),
        in_specs=[pl.BlockSpec((tm, k), lambda i, j: (i, 0)), pl.BlockSpec((k, tn), lambda i, j: (0, j))],
        out_specs=pl.BlockSpec((tm, tn), lambda i, j: (i, j)),
        out_shape=jax.ShapeDtypeStruct((m, n), BF16),
        scratch_shapes=[pltpu.VMEM((tm, k), BF16)],
        compiler_params=_params("arbitrary", "arbitrary"),
        name="matmul",
    )(a, b)


def _split3(v):
    hi = v.astype(BF16).astype(F32)
    r1 = v - hi
    mid = r1.astype(BF16).astype(F32)
    lo = (r1 - mid).astype(BF16).astype(F32)
    return hi, mid, lo


def _fox_gate_kernel(x_ref, w_ref, b_ref, tri_ref, aq_ref, ak_ref, carry):
    t = pl.program_id(1)

    @pl.when(t == 0)
    def _():
        carry[...] = jnp.zeros_like(carry)

    logit = _dot(x_ref[...].astype(BF16), w_ref[...]) + b_ref[...]
    log_f = jnp.minimum(logit, 0.0) - jnp.log1p(jnp.exp(-jnp.abs(logit)))
    tri = tri_ref[...]
    hi, mid, lo = _split3(log_f)
    c = (_dot(tri, hi.astype(BF16)) + _dot(tri, mid.astype(BF16))) + _dot(tri, lo.astype(BF16))
    c = c + carry[0:1, :]
    ts = c.shape[0]
    carry[0:1, :] = c[ts - 1:ts, :]
    hi, mid, lo = _split3(c * LOG2E)
    group = lax.broadcasted_iota(jnp.int32, c.shape, 1) // FOX_HEADS
    pieces = jnp.where(group % 3 == 0, hi, jnp.where(group % 3 == 1, mid, lo))
    aq_ref[...] = jnp.where(group < 3, pieces, jnp.where(group < 6, 1.0, 0.0)).astype(BF16)
    ak_ref[...] = jnp.where(group < 3, 1.0, jnp.where(group < 6, -pieces, 0.0)).astype(BF16)


def _fox_gate(x, w, b, batch, ts=512):
    n, d = x.shape
    steps = n // batch // ts
    tri = jnp.asarray(np.tril(np.ones((ts, ts), np.float32)), BF16)
    row = lambda bi, t: (bi * steps + t, 0)
    fixed = lambda bi, t: (0, 0)
    return pl.pallas_call(
        _fox_gate_kernel,
        grid=(batch, steps),
        in_specs=[pl.BlockSpec((ts, d), row), pl.BlockSpec(w.shape, fixed),
                  pl.BlockSpec((1, LANES), fixed), pl.BlockSpec((ts, ts), fixed)],
        out_specs=[pl.BlockSpec((ts, LANES), row), pl.BlockSpec((ts, LANES), row)],
        out_shape=[jax.ShapeDtypeStruct((n, LANES), BF16), jax.ShapeDtypeStruct((n, LANES), BF16)],
        scratch_shapes=[pltpu.VMEM((8, LANES), F32)],
        compiler_params=_params("arbitrary", "arbitrary"),
        name="fox_gate",
    )(x, w, b, tri)


def _attn_kernel(qm_ref, qa_ref, lm_ref, km_ref, ka_ref, v_ref, bias_ref, o_ref,
                 kcat, vcat, m_sc, acc_sc, *, tq, hpb):
    i = pl.program_id(2)

    @pl.when(i == 0)
    def _():
        lane = lax.broadcasted_iota(jnp.int32, ka_ref.shape, 1)
        ones_col = jnp.where(lane == 0, 1.0, 0.0).astype(BF16)
        for g in range(hpb):
            head = slice(g * LANES, (g + 1) * LANES)
            kcat[g, :, :LANES] = km_ref[:, head]
            kcat[g, :, LANES:] = ka_ref[...]
            vcat[g, :, :LANES] = v_ref[:, head]
            vcat[g, :, LANES:] = ones_col

    qa_tiles = qa_ref.shape[1] // LANES

    def q_tile(g):
        t = g * qa_tiles // hpb
        aux = qa_ref[:, t * LANES:(t + 1) * LANES] * lm_ref[g, 0:1, :]
        return jnp.concatenate([qm_ref[:, g * LANES:(g + 1) * LANES], aux], axis=1)

    q = [q_tile(g) for g in range(hpb)]

    def tile(g, j, bias):
        rows = pl.ds(pl.multiple_of(j * tq, tq), tq)
        s = lax.dot_general(q[g], kcat[g, rows, :], (((1,), (1,)), ((), ())), preferred_element_type=F32)
        if bias is not None:
            s = s + bias
        mx = jnp.broadcast_to(jnp.max(s, axis=-1, keepdims=True), (tq, LANES))
        m_new = mx if bias is not None else jnp.maximum(m_sc[g], mx)
        p = jnp.exp2(s - jnp.tile(m_new, (1, tq // LANES))).astype(BF16)
        pv = _dot(p, vcat[g, rows, :])
        if bias is not None:
            acc_sc[g] = pv
        else:
            acc_sc[g] = jnp.tile(jnp.exp2(m_sc[g] - m_new), (1, 2)) * acc_sc[g] + pv
        m_sc[g] = m_new

    bias = bias_ref[...]
    for g in range(hpb):
        tile(g, i, bias)

    def body(j, carry):
        for g in range(hpb):
            tile(g, j, None)
        return carry

    lax.fori_loop(0, i, body, 0)
    for g in range(hpb):
        acc = acc_sc[g]
        o_ref[:, g * LANES:(g + 1) * LANES] = (acc[:, :LANES] / acc[:, LANES:LANES + 1]).astype(o_ref.dtype)


def _attention(qm, qm_col0, qa, qa_tiles, lane_mask, km, km_col0, ka, v, v_col0, bias, batch, heads,
               tq=512, hpb=ATTN_HEADS_PER_STEP):
    n = qm.shape[0]
    seq = n // batch
    nq = seq // tq
    wide = hpb * LANES
    q_row = lambda b, i: b * nq + i
    return pl.pallas_call(
        functools.partial(_attn_kernel, tq=tq, hpb=hpb),
        grid=(batch, heads // hpb, nq),
        in_specs=[
            pl.BlockSpec((tq, wide), lambda b, h, i: (q_row(b, i), qm_col0 // hpb + h)),
            pl.BlockSpec((tq, max(qa_tiles, 1) * LANES), lambda b, h, i: (q_row(b, i), h if qa_tiles else 0)),
            pl.BlockSpec((hpb, 8, LANES), lambda b, h, i: (h, 0, 0)),
            pl.BlockSpec((seq, wide), lambda b, h, i: (b, km_col0 // hpb + h)),
            pl.BlockSpec((seq, LANES), lambda b, h, i: (b, 0)),
            pl.BlockSpec((seq, wide), lambda b, h, i: (b, v_col0 // hpb + h)),
            pl.BlockSpec((tq, tq), lambda b, h, i: (0, 0)),
        ],
        out_specs=pl.BlockSpec((tq, wide), lambda b, h, i: (q_row(b, i), h)),
        out_shape=jax.ShapeDtypeStruct((n, heads * LANES), BF16),
        scratch_shapes=[pltpu.VMEM((hpb, seq, 2 * LANES), BF16), pltpu.VMEM((hpb, seq, 2 * LANES), BF16),
                        pltpu.VMEM((hpb, tq, LANES), F32), pltpu.VMEM((hpb, tq, 2 * LANES), F32)],
        compiler_params=_params("arbitrary", "arbitrary", "arbitrary"),
        name="attention",
    )(qm, qa, lane_mask, km, ka, v, bias)


def _proj_ln_kernel(o_ref, w_ref, x_ref, g_ref, b_ref, y_ref, *, alpha):
    y = alpha * x_ref[...] + _dot(o_ref[...], w_ref[...])
    y_ref[...] = _layer_norm(y, g_ref[...], b_ref[...])


def _proj_ln(o, w, x, g, b, alpha, tm=512):
    n, d = x.shape
    row = lambda i: (i, 0)
    fixed = lambda i: (0, 0)
    return pl.pallas_call(
        functools.partial(_proj_ln_kernel, alpha=alpha),
        grid=(n // tm,),
        in_specs=[pl.BlockSpec((tm, o.shape[1]), row), pl.BlockSpec(w.shape, fixed),
                  pl.BlockSpec((tm, d), row), pl.BlockSpec((1, d), fixed), pl.BlockSpec((1, d), fixed)],
        out_specs=pl.BlockSpec((tm, d), row),
        out_shape=jax.ShapeDtypeStruct((n, d), F32),
        compiler_params=_params("arbitrary"),
        name="proj_ln",
    )(o, w, x, g, b)


def _ffn_kernel(x_ref, wg_ref, wu_ref, wd_ref, g_ref, b_ref, y_ref, x_bf, acc, *, alpha):
    j = pl.program_id(1)

    @pl.when(j == 0)
    def _():
        x_bf[...] = x_ref[...].astype(BF16)

    xb = x_bf[...]
    gate = _dot(xb, wg_ref[...])
    up = _dot(xb, wu_ref[...])
    part = _dot((gate * jax.nn.sigmoid(gate) * up).astype(BF16), wd_ref[...])

    @pl.when(j == 0)
    def _():
        acc[...] = part

    @pl.when(j > 0)
    def _():
        acc[...] += part

    @pl.when(j == pl.num_programs(1) - 1)
    def _():
        y_ref[...] = _layer_norm(alpha * x_ref[...] + acc[...], g_ref[...], b_ref[...])


def _ffn(x, w_gu, w_down, g, b, alpha, tm=512, tf=512):
    n, d = x.shape
    hidden = w_down.shape[0]
    nf = hidden // tf
    return pl.pallas_call(
        functools.partial(_ffn_kernel, alpha=alpha),
        grid=(n // tm, nf),
        in_specs=[pl.BlockSpec((tm, d), lambda i, j: (i, 0)),
                  pl.BlockSpec((d, tf), lambda i, j: (0, j)),
                  pl.BlockSpec((d, tf), lambda i, j: (0, nf + j)),
                  pl.BlockSpec((tf, d), lambda i, j: (j, 0)),
                  pl.BlockSpec((1, d), lambda i, j: (0, 0)), pl.BlockSpec((1, d), lambda i, j: (0, 0))],
        out_specs=pl.BlockSpec((tm, d), lambda i, j: (i, 0)),
        out_shape=jax.ShapeDtypeStruct((n, d), F32),
        scratch_shapes=[pltpu.VMEM((tm, d), BF16), pltpu.VMEM((tm, d), F32)],
        compiler_params=_params("arbitrary", "arbitrary"),
        name="ffn",
    )(x, w_gu, w_gu, w_down, g, b)


def _mla_weights(w_in, w_q_up, w_kv_up):
    d = w_in.shape[0]
    half = MLA_ROPE_DIM // 2
    base = MLA_Q_RANK + MLA_KV_RANK
    k1, k2 = w_in[:, base:base + half], w_in[:, base + half:]
    w_in_cat = jnp.concatenate([w_in[:, :base], k1, k2, k1, k2, -k2, k1, -k2, k1], axis=1)
    wq = w_q_up.reshape(MLA_Q_RANK, MLA_HEADS, MLA_NOPE_DIM + MLA_ROPE_DIM)
    r1 = wq[:, :, MLA_NOPE_DIM:MLA_NOPE_DIM + half]
    r2 = wq[:, :, MLA_NOPE_DIM + half:]
    w_q_cat = jnp.concatenate([
        wq[:, :, :MLA_NOPE_DIM].reshape(MLA_Q_RANK, -1),
        wq[:, :, MLA_NOPE_DIM:].reshape(MLA_Q_RANK, -1),
        jnp.concatenate([-r2, r1], axis=2).reshape(MLA_Q_RANK, -1)], axis=1)
    wkv = w_kv_up.reshape(MLA_KV_RANK, MLA_HEADS, MLA_NOPE_DIM + MLA_V_DIM)
    w_kv_cat = jnp.concatenate([wkv[:, :, :MLA_NOPE_DIM].reshape(MLA_KV_RANK, -1),
                                wkv[:, :, MLA_NOPE_DIM:].reshape(MLA_KV_RANK, -1)], axis=1)
    del d
    return w_in_cat.astype(BF16), w_q_cat.astype(BF16), w_kv_cat.astype(BF16)


def _causal_bias(tq, granularity):
    r = np.arange(tq)[:, None] // granularity
    c = np.arange(tq)[None, :] // granularity
    return jnp.asarray(np.where(c <= r, 0.0, NEG_INF), F32)


def _mla_lane_mask():
    lane = np.arange(LANES)[None, :]
    head = np.arange(MLA_HEADS)[:, None]
    m = (lane // MLA_ROPE_DIM == head % 2).astype(np.float32)
    return jnp.asarray(np.broadcast_to(m[:, None, :], (MLA_HEADS, 8, LANES)), BF16)


def _fox_lane_mask():
    lane = np.arange(LANES)[None, :]
    head = np.arange(FOX_HEADS)[:, None]
    m = ((lane % FOX_HEADS == head) & (lane < 6 * FOX_HEADS)).astype(np.float32)
    return jnp.asarray(np.broadcast_to(m[:, None, :], (FOX_HEADS, 8, LANES)), BF16)


def _mla_layer(x, pos, w_in, q_norm_g, w_q_up, kv_norm_g, w_kv_up, w_o, batch):
    w_in_cat, w_q_cat, w_kv_cat = _mla_weights(w_in, w_q_up, w_kv_up)
    half = MLA_ROPE_DIM // 2
    inv_freq = ROPE_THETA ** (-jnp.arange(0, MLA_ROPE_DIM, 2, dtype=F32) / MLA_ROPE_DIM)
    freq = jnp.tile(inv_freq, LANES // half)[None, :]
    cq, ckv, kr, cos, sin = _mla_in(x, pos, freq, w_in_cat, q_norm_g[None, :], kv_norm_g[None, :])
    scale = (MLA_NOPE_DIM + MLA_ROPE_DIM) ** -0.5 * LOG2E
    qn, qr = _mla_q(cq, cos, sin, w_q_cat, scale)
    kv = _matmul(ckv, w_kv_cat, tm=1024, tn=1024)
    o = _attention(qn, 0, qr, ATTN_HEADS_PER_STEP // 2, _mla_lane_mask(), kv, 0, kr, kv, MLA_HEADS,
                   _causal_bias(512, CHUNK), batch, MLA_HEADS)
    return o, w_o.astype(BF16)


def _fox_layer(x, w_in, b_f, w_o, batch):
    width = FOX_HEADS * FOX_HEAD_DIM
    qkv = _matmul(x, w_in[:, :3 * width].astype(BF16), tm=1024, tn=1024,
                  n_scaled=width // 1024, scale=FOX_HEAD_DIM ** -0.5 * LOG2E)
    groups = 6
    w_f = jnp.pad(jnp.tile(w_in[:, 3 * width:], (1, groups)), ((0, 0), (0, LANES - groups * FOX_HEADS)))
    b6 = jnp.pad(jnp.tile(b_f, groups), (0, LANES - groups * FOX_HEADS))[None, :]
    aq, ak = _fox_gate(x, w_f.astype(BF16), b6, batch)
    o = _attention(qkv, 0, aq, 0, _fox_lane_mask(), qkv, FOX_HEADS, ak, qkv, 2 * FOX_HEADS,
                   _causal_bias(512, 1), batch, FOX_HEADS)
    return o, w_o.astype(BF16)


def kernel(x, positions, mla_w_in, mla_q_norm_g, mla_w_q_up, mla_kv_norm_g, mla_w_kv_up, mla_w_o,
           fox_w_in, fox_b_f, fox_w_o, ffn_w_gu, ffn_w_down, ln_mix_g, ln_mix_b, ln_ffn_g, ln_ffn_b):
    batch, seq, d = x.shape
    depth = ffn_w_gu.shape[0]
    alpha = float((2 * depth) ** 0.25)
    h = x.reshape(batch * seq, d)
    pos = positions.astype(F32).reshape(batch * seq, 1)
    for i in range(depth):
        j = i // 2
        if i % 2 == 0:
            o, w_o = _mla_layer(h, pos, mla_w_in[j], mla_q_norm_g[j], mla_w_q_up[j], mla_kv_norm_g[j],
                                mla_w_kv_up[j], mla_w_o[j], batch)
        else:
            o, w_o = _fox_layer(h, fox_w_in[j], fox_b_f[j], fox_w_o[j], batch)
        h = _proj_ln(o, w_o, h, ln_mix_g[i][None, :], ln_mix_b[i][None, :], alpha)
        h = _ffn(h, ffn_w_gu[i].astype(BF16), ffn_w_down[i].astype(BF16),
                 ln_ffn_g[i][None, :], ln_ffn_b[i][None, :], alpha)
    return h.reshape(batch, seq, d)
```

```python
import functools

import jax
import jax.numpy as jnp
import numpy as np
from jax import lax
from jax.experimental import pallas as pl
from jax.experimental.pallas import tpu as pltpu

CHUNK = 64
MLA_HEADS = 16
MLA_NOPE_DIM = 128
MLA_ROPE_DIM = 64
MLA_V_DIM = 128
MLA_Q_RANK = 512
MLA_KV_RANK = 512
ROPE_THETA = 10000.0
FOX_HEADS = 16
FOX_HEAD_DIM = 128
LN_EPS = 1e-5
RMS_EPS = 1e-6
NEG_INF = -1e30
LOG2E = 1.4426950408889634

LANES = 128
VMEM_LIMIT_BYTES = 56 * 1024 * 1024
ATTN_HEADS_PER_STEP = 4

BF16 = jnp.bfloat16
F32 = jnp.float32


def _params(*semantics):
    return pltpu.CompilerParams(dimension_semantics=semantics, vmem_limit_bytes=VMEM_LIMIT_BYTES)


def _dot(a, b):
    return jnp.dot(a, b, preferred_element_type=F32)


def _layer_norm(y, g, b):
    mu = jnp.mean(y, axis=-1, keepdims=True)
    d = y - mu
    var = jnp.mean(d * d, axis=-1, keepdims=True)
    return d * lax.rsqrt(var + LN_EPS) * g + b


def _rms_norm(y, g):
    return y * lax.rsqrt(jnp.mean(y * y, axis=-1, keepdims=True) + RMS_EPS) * g


def _mla_in_kernel(x_ref, pos_ref, freq_ref, w_ref, gq_ref, gkv_ref,
                   cq_ref, ckv_ref, kr_ref, cos_ref, sin_ref):
    h = _dot(x_ref[...].astype(BF16), w_ref[...])
    cq_ref[...] = _rms_norm(h[:, :MLA_Q_RANK], gq_ref[...]).astype(BF16)
    ckv_ref[...] = _rms_norm(h[:, MLA_Q_RANK:MLA_Q_RANK + MLA_KV_RANK], gkv_ref[...]).astype(BF16)
    ang = pos_ref[...] * freq_ref[...]
    c = jnp.cos(ang)
    s = jnp.sin(ang)
    base = MLA_Q_RANK + MLA_KV_RANK
    kr_ref[...] = (h[:, base:base + LANES] * c + h[:, base + LANES:] * s).astype(BF16)
    cos_ref[...] = c
    sin_ref[...] = s


def _mla_in(x, pos, freq, w, gq, gkv, tm=512):
    n, d = x.shape
    row = lambda i: (i, 0)
    fixed = lambda i: (0, 0)
    return pl.pallas_call(
        _mla_in_kernel,
        grid=(n // tm,),
        in_specs=[pl.BlockSpec((tm, d), row), pl.BlockSpec((tm, 1), row),
                  pl.BlockSpec((1, LANES), fixed), pl.BlockSpec(w.shape, fixed),
                  pl.BlockSpec((1, MLA_Q_RANK), fixed), pl.BlockSpec((1, MLA_KV_RANK), fixed)],
        out_specs=[pl.BlockSpec((tm, MLA_Q_RANK), row), pl.BlockSpec((tm, MLA_KV_RANK), row),
                   pl.BlockSpec((tm, LANES), row), pl.BlockSpec((tm, LANES), row),
                   pl.BlockSpec((tm, LANES), row)],
        out_shape=[jax.ShapeDtypeStruct((n, MLA_Q_RANK), BF16),
                   jax.ShapeDtypeStruct((n, MLA_KV_RANK), BF16),
                   jax.ShapeDtypeStruct((n, LANES), BF16),
                   jax.ShapeDtypeStruct((n, LANES), F32),
                   jax.ShapeDtypeStruct((n, LANES), F32)],
        compiler_params=_params("arbitrary"),
        name="mla_in",
    )(x, pos, freq, w, gq, gkv)


def _mla_q_kernel(cq_ref, cos_ref, sin_ref, w_ref, qn_ref, qr_ref, *, scale):
    n_nope = MLA_HEADS * MLA_NOPE_DIM
    n_rope = MLA_HEADS * MLA_ROPE_DIM
    cq = cq_ref[...]
    qn_ref[...] = (_dot(cq, w_ref[:, :n_nope]) * scale).astype(BF16)
    r = _dot(cq, w_ref[:, n_nope:n_nope + n_rope])
    rp = _dot(cq, w_ref[:, n_nope + n_rope:])
    c = cos_ref[...]
    s = sin_ref[...]
    for p in range(n_rope // LANES):
        sl = slice(p * LANES, (p + 1) * LANES)
        qr_ref[:, sl] = ((r[:, sl] * c + rp[:, sl] * s) * scale).astype(BF16)


def _mla_q(cq, cos, sin, w, scale, tm=512):
    n = cq.shape[0]
    n_nope = MLA_HEADS * MLA_NOPE_DIM
    n_rope = MLA_HEADS * MLA_ROPE_DIM
    row = lambda i: (i, 0)
    fixed = lambda i: (0, 0)
    return pl.pallas_call(
        functools.partial(_mla_q_kernel, scale=scale),
        grid=(n // tm,),
        in_specs=[pl.BlockSpec((tm, MLA_Q_RANK), row), pl.BlockSpec((tm, LANES), row),
                  pl.BlockSpec((tm, LANES), row), pl.BlockSpec(w.shape, fixed)],
        out_specs=[pl.BlockSpec((tm, n_nope), row), pl.BlockSpec((tm, n_rope), row)],
        out_shape=[jax.ShapeDtypeStruct((n, n_nope), BF16), jax.ShapeDtypeStruct((n, n_rope), BF16)],
        compiler_params=_params("arbitrary"),
        name="mla_q_up",
    )(cq, cos, sin, w)


def _matmul_kernel(a_ref, b_ref, o_ref, a_bf, *, n_scaled, scale):
    j = pl.program_id(1)

    @pl.when(j == 0)
    def _():
        a_bf[...] = a_ref[...].astype(BF16)

    acc = _dot(a_bf[...], b_ref[...])
    if n_scaled:
        acc = acc * jnp.where(j < n_scaled, scale, 1.0).astype(F32)
    o_ref[...] = acc.astype(o_ref.dtype)


def _matmul(a, b, tm, tn, n_scaled=0, scale=1.0):
    m, k = a.shape
    n = b.shape[1]
    return pl.pallas_call(
        functools.partial(_matmul_kernel, n_scaled=n_scaled, scale=scale),
        grid=(m // tm, n // tn),
        in_specs=[pl.BlockSpec((tm, k), lambda i, j: (i, 0)), pl.BlockSpec((k, tn), lambda i, j: (0, j))],
        out_specs=pl.BlockSpec((tm, tn), lambda i, j: (i, j)),
        out_shape=jax.ShapeDtypeStruct((m, n), BF16),
        scratch_shapes=[pltpu.VMEM((tm, k), BF16)],
        compiler_params=_params("arbitrary", "arbitrary"),
        name="matmul",
    )(a, b)


def _split3(v):
    hi = v.astype(BF16).astype(F32)
    r1 = v - hi
    mid = r1.astype(BF16).astype(F32)
    lo = (r1 - mid).astype(BF16).astype(F32)
    return hi, mid, lo


def _fox_gate_kernel(x_ref, w_ref, b_ref, tri_ref, aq_ref, ak_ref, carry):
    t = pl.program_id(1)

    @pl.when(t == 0)
    def _():
        carry[...] = jnp.zeros_like(carry)

    logit = _dot(x_ref[...].astype(BF16), w_ref[...]) + b_ref[...]
    log_f = jnp.minimum(logit, 0.0) - jnp.log1p(jnp.exp(-jnp.abs(logit)))
    tri = tri_ref[...]
    hi, mid, lo = _split3(log_f)
    c = (_dot(tri, hi.astype(BF16)) + _dot(tri, mid.astype(BF16))) + _dot(tri, lo.astype(BF16))
    c = c + carry[0:1, :]
    ts = c.shape[0]
    carry[0:1, :] = c[ts - 1:ts, :]
    hi, mid, lo = _split3(c * LOG2E)
    group = lax.broadcasted_iota(jnp.int32, c.shape, 1) // FOX_HEADS
    pieces = jnp.where(group % 3 == 0, hi, jnp.where(group % 3 == 1, mid, lo))
    aq_ref[...] = jnp.where(group < 3, pieces, jnp.where(group < 6, 1.0, 0.0)).astype(BF16)
    ak_ref[...] = jnp.where(group < 3, 1.0, jnp.where(group < 6, -pieces, 0.0)).astype(BF16)


def _fox_gate(x, w, b, batch, ts=512):
    n, d = x.shape
    steps = n // batch // ts
    tri = jnp.asarray(np.tril(np.ones((ts, ts), np.float32)), BF16)
    row = lambda bi, t: (bi * steps + t, 0)
    fixed = lambda bi, t: (0, 0)
    return pl.pallas_call(
        _fox_gate_kernel,
        grid=(batch, steps),
        in_specs=[pl.BlockSpec((ts, d), row), pl.BlockSpec(w.shape, fixed),
                  pl.BlockSpec((1, LANES), fixed), pl.BlockSpec((ts, ts), fixed)],
        out_specs=[pl.BlockSpec((ts, LANES), row), pl.BlockSpec((ts, LANES), row)],
        out_shape=[jax.ShapeDtypeStruct((n, LANES), BF16), jax.ShapeDtypeStruct((n, LANES), BF16)],
        scratch_shapes=[pltpu.VMEM((8, LANES), F32)],
        compiler_params=_params("arbitrary", "arbitrary"),
        name="fox_gate",
    )(x, w, b, tri)


def _attn_kernel(qm_ref, qa_ref, lm_ref, km_ref, ka_ref, v_ref, bias_ref, o_ref,
                 kcat, vcat, m_sc, acc_sc, *, tq, hpb):
    i = pl.program_id(2)

    @pl.when(i == 0)
    def _():
        ones = jnp.ones(ka_ref.shape, BF16)
        for g in range(hpb):
            head = slice(g * LANES, (g + 1) * LANES)
            kcat[g, :, :LANES] = km_ref[:, head]
            kcat[g, :, LANES:] = ka_ref[...]
            vcat[g, :, :LANES] = v_ref[:, head]
            vcat[g, :, LANES:] = ones

    qa_tiles = qa_ref.shape[1] // LANES

    def q_tile(g):
        t = g * qa_tiles // hpb
        aux = qa_ref[:, t * LANES:(t + 1) * LANES] * lm_ref[g, 0:1, :]
        return jnp.concatenate([qm_ref[:, g * LANES:(g + 1) * LANES], aux], axis=1)

    q = [q_tile(g) for g in range(hpb)]

    def tile(g, j, bias):
        rows = pl.ds(pl.multiple_of(j * tq, tq), tq)
        s = lax.dot_general(q[g], kcat[g, rows, :], (((1,), (1,)), ((), ())), preferred_element_type=F32)
        if bias is not None:
            s = s + bias
        mx = jnp.broadcast_to(jnp.max(s, axis=-1, keepdims=True), (tq, LANES))
        m_new = jnp.maximum(m_sc[g], mx)
        p = jnp.exp2(s - jnp.tile(m_new, (1, tq // LANES))).astype(BF16)
        pv = _dot(p, vcat[g, rows, :])
        acc_sc[g] = jnp.tile(jnp.exp2(m_sc[g] - m_new), (1, 2)) * acc_sc[g] + pv
        m_sc[g] = m_new

    def tiles(js_and_biases):
        for j, bias in js_and_biases:
            for g in range(hpb):
                tile(g, j, bias)

    m_sc[...] = jnp.full(m_sc.shape, -jnp.inf, F32)
    acc_sc[...] = jnp.zeros_like(acc_sc)
    odd = i % 2 == 1

    @pl.when(odd)
    def _():
        tiles([(i, bias_ref[...]), (i - 1, None)])

    @pl.when(jnp.logical_not(odd))
    def _():
        tiles([(i, bias_ref[...])])

    def body(jj, carry):
        tiles([(2 * jj, None), (2 * jj + 1, None)])
        return carry

    lax.fori_loop(0, i // 2, body, 0)
    for g in range(hpb):
        acc = acc_sc[g]
        o_ref[:, g * LANES:(g + 1) * LANES] = (acc[:, :LANES] / acc[:, LANES:]).astype(o_ref.dtype)


def _attention(qm, qm_col0, qa, qa_tiles, lane_mask, km, km_col0, ka, v, v_col0, bias, batch, heads,
               tq=512, hpb=ATTN_HEADS_PER_STEP):
    n = qm.shape[0]
    seq = n // batch
    nq = seq // tq
    wide = hpb * LANES
    q_row = lambda b, i: b * nq + i
    return pl.pallas_call(
        functools.partial(_attn_kernel, tq=tq, hpb=hpb),
        grid=(batch, heads // hpb, nq),
        in_specs=[
            pl.BlockSpec((tq, wide), lambda b, h, i: (q_row(b, i), qm_col0 // hpb + h)),
            pl.BlockSpec((tq, max(qa_tiles, 1) * LANES), lambda b, h, i: (q_row(b, i), h if qa_tiles else 0)),
            pl.BlockSpec((hpb, 8, LANES), lambda b, h, i: (h, 0, 0)),
            pl.BlockSpec((seq, wide), lambda b, h, i: (b, km_col0 // hpb + h)),
            pl.BlockSpec((seq, LANES), lambda b, h, i: (b, 0)),
            pl.BlockSpec((seq, wide), lambda b, h, i: (b, v_col0 // hpb + h)),
            pl.BlockSpec((tq, tq), lambda b, h, i: (0, 0)),
        ],
        out_specs=pl.BlockSpec((tq, wide), lambda b, h, i: (q_row(b, i), h)),
        out_shape=jax.ShapeDtypeStruct((n, heads * LANES), BF16),
        scratch_shapes=[pltpu.VMEM((hpb, seq, 2 * LANES), BF16), pltpu.VMEM((hpb, seq, 2 * LANES), BF16),
                        pltpu.VMEM((hpb, tq, LANES), F32), pltpu.VMEM((hpb, tq, 2 * LANES), F32)],
        compiler_params=_params("arbitrary", "arbitrary", "arbitrary"),
        name="attention",
    )(qm, qa, lane_mask, km, ka, v, bias)


def _proj_ln_kernel(o_ref, w_ref, x_ref, g_ref, b_ref, y_ref, *, alpha):
    y = alpha * x_ref[...] + _dot(o_ref[...], w_ref[...])
    y_ref[...] = _layer_norm(y, g_ref[...], b_ref[...])


def _proj_ln(o, w, x, g, b, alpha, tm=512):
    n, d = x.shape
    row = lambda i: (i, 0)
    fixed = lambda i: (0, 0)
    return pl.pallas_call(
        functools.partial(_proj_ln_kernel, alpha=alpha),
        grid=(n // tm,),
        in_specs=[pl.BlockSpec((tm, o.shape[1]), row), pl.BlockSpec(w.shape, fixed),
                  pl.BlockSpec((tm, d), row), pl.BlockSpec((1, d), fixed), pl.BlockSpec((1, d), fixed)],
        out_specs=pl.BlockSpec((tm, d), row),
        out_shape=jax.ShapeDtypeStruct((n, d), F32),
        compiler_params=_params("arbitrary"),
        name="proj_ln",
    )(o, w, x, g, b)


def _ffn_kernel(x_ref, wg_ref, wu_ref, wd_ref, g_ref, b_ref, y_ref, x_bf, acc, *, alpha):
    j = pl.program_id(1)

    @pl.when(j == 0)
    def _():
        x_bf[...] = x_ref[...].astype(BF16)
        acc[...] = jnp.zeros_like(acc)

    xb = x_bf[...]
    gate = _dot(xb, wg_ref[...])
    up = _dot(xb, wu_ref[...])
    acc[...] += _dot((gate * jax.nn.sigmoid(gate) * up).astype(BF16), wd_ref[...])

    @pl.when(j == pl.num_programs(1) - 1)
    def _():
        y_ref[...] = _layer_norm(alpha * x_ref[...] + acc[...], g_ref[...], b_ref[...])


def _ffn(x, w_gu, w_down, g, b, alpha, tm=512, tf=512):
    n, d = x.shape
    hidden = w_down.shape[0]
    nf = hidden // tf
    return pl.pallas_call(
        functools.partial(_ffn_kernel, alpha=alpha),
        grid=(n // tm, nf),
        in_specs=[pl.BlockSpec((tm, d), lambda i, j: (i, 0)),
                  pl.BlockSpec((d, tf), lambda i, j: (0, j)),
                  pl.BlockSpec((d, tf), lambda i, j: (0, nf + j)),
                  pl.BlockSpec((tf, d), lambda i, j: (j, 0)),
                  pl.BlockSpec((1, d), lambda i, j: (0, 0)), pl.BlockSpec((1, d), lambda i, j: (0, 0))],
        out_specs=pl.BlockSpec((tm, d), lambda i, j: (i, 0)),
        out_shape=jax.ShapeDtypeStruct((n, d), F32),
        scratch_shapes=[pltpu.VMEM((tm, d), BF16), pltpu.VMEM((tm, d), F32)],
        compiler_params=_params("arbitrary", "arbitrary"),
        name="ffn",
    )(x, w_gu, w_gu, w_down, g, b)


def _mla_weights(w_in, w_q_up, w_kv_up):
    d = w_in.shape[0]
    half = MLA_ROPE_DIM // 2
    base = MLA_Q_RANK + MLA_KV_RANK
    k1, k2 = w_in[:, base:base + half], w_in[:, base + half:]
    w_in_cat = jnp.concatenate([w_in[:, :base], k1, k2, k1, k2, -k2, k1, -k2, k1], axis=1)
    wq = w_q_up.reshape(MLA_Q_RANK, MLA_HEADS, MLA_NOPE_DIM + MLA_ROPE_DIM)
    r1 = wq[:, :, MLA_NOPE_DIM:MLA_NOPE_DIM + half]
    r2 = wq[:, :, MLA_NOPE_DIM + half:]
    w_q_cat = jnp.concatenate([
        wq[:, :, :MLA_NOPE_DIM].reshape(MLA_Q_RANK, -1),
        wq[:, :, MLA_NOPE_DIM:].reshape(MLA_Q_RANK, -1),
        jnp.concatenate([-r2, r1], axis=2).reshape(MLA_Q_RANK, -1)], axis=1)
    wkv = w_kv_up.reshape(MLA_KV_RANK, MLA_HEADS, MLA_NOPE_DIM + MLA_V_DIM)
    w_kv_cat = jnp.concatenate([wkv[:, :, :MLA_NOPE_DIM].reshape(MLA_KV_RANK, -1),
                                wkv[:, :, MLA_NOPE_DIM:].reshape(MLA_KV_RANK, -1)], axis=1)
    del d
    return w_in_cat.astype(BF16), w_q_cat.astype(BF16), w_kv_cat.astype(BF16)


def _causal_bias(tq, granularity):
    r = np.arange(tq)[:, None] // granularity
    c = np.arange(tq)[None, :] // granularity
    return jnp.asarray(np.where(c <= r, 0.0, NEG_INF), F32)


def _mla_lane_mask():
    lane = np.arange(LANES)[None, :]
    head = np.arange(MLA_HEADS)[:, None]
    m = (lane // MLA_ROPE_DIM == head % 2).astype(np.float32)
    return jnp.asarray(np.broadcast_to(m[:, None, :], (MLA_HEADS, 8, LANES)), BF16)


def _fox_lane_mask():
    lane = np.arange(LANES)[None, :]
    head = np.arange(FOX_HEADS)[:, None]
    m = ((lane % FOX_HEADS == head) & (lane < 6 * FOX_HEADS)).astype(np.float32)
    return jnp.asarray(np.broadcast_to(m[:, None, :], (FOX_HEADS, 8, LANES)), BF16)


def _mla_layer(x, pos, w_in, q_norm_g, w_q_up, kv_norm_g, w_kv_up, w_o, batch):
    w_in_cat, w_q_cat, w_kv_cat = _mla_weights(w_in, w_q_up, w_kv_up)
    half = MLA_ROPE_DIM // 2
    inv_freq = ROPE_THETA ** (-jnp.arange(0, MLA_ROPE_DIM, 2, dtype=F32) / MLA_ROPE_DIM)
    freq = jnp.tile(inv_freq, LANES // half)[None, :]
    cq, ckv, kr, cos, sin = _mla_in(x, pos, freq, w_in_cat, q_norm_g[None, :], kv_norm_g[None, :])
    scale = (MLA_NOPE_DIM + MLA_ROPE_DIM) ** -0.5 * LOG2E
    qn, qr = _mla_q(cq, cos, sin, w_q_cat, scale)
    kv = _matmul(ckv, w_kv_cat, tm=1024, tn=1024)
    o = _attention(qn, 0, qr, ATTN_HEADS_PER_STEP // 2, _mla_lane_mask(), kv, 0, kr, kv, MLA_HEADS,
                   _causal_bias(512, CHUNK), batch, MLA_HEADS)
    return o, w_o.astype(BF16)


def _fox_layer(x, w_in, b_f, w_o, batch):
    width = FOX_HEADS * FOX_HEAD_DIM
    qkv = _matmul(x, w_in[:, :3 * width].astype(BF16), tm=1024, tn=1024,
                  n_scaled=width // 1024, scale=FOX_HEAD_DIM ** -0.5 * LOG2E)
    groups = 6
    w_f = jnp.pad(jnp.tile(w_in[:, 3 * width:], (1, groups)), ((0, 0), (0, LANES - groups * FOX_HEADS)))
    b6 = jnp.pad(jnp.tile(b_f, groups), (0, LANES - groups * FOX_HEADS))[None, :]
    aq, ak = _fox_gate(x, w_f.astype(BF16), b6, batch)
    o = _attention(qkv, 0, aq, 0, _fox_lane_mask(), qkv, FOX_HEADS, ak, qkv, 2 * FOX_HEADS,
                   _causal_bias(512, 1), batch, FOX_HEADS)
    return o, w_o.astype(BF16)


def kernel(x, positions, mla_w_in, mla_q_norm_g, mla_w_q_up, mla_kv_norm_g, mla_w_kv_up, mla_w_o,
           fox_w_in, fox_b_f, fox_w_o, ffn_w_gu, ffn_w_down, ln_mix_g, ln_mix_b, ln_ffn_g, ln_ffn_b):
    batch, seq, d = x.shape
    depth = ffn_w_gu.shape[0]
    alpha = float((2 * depth) ** 0.25)
    h = x.reshape(batch * seq, d)
    pos = positions.astype(F32).reshape(batch * seq, 1)
    for i in range(depth):
        j = i // 2
        if i % 2 == 0:
            o, w_o = _mla_layer(h, pos, mla_w_in[j], mla_q_norm_g[j], mla_w_q_up[j], mla_kv_norm_g[j],
                                mla_w_kv_up[j], mla_w_o[j], batch)
        else:
            o, w_o = _fox_layer(h, fox_w_in[j], fox_b_f[j], fox_w_o[j], batch)
        h = _proj_ln(o, w_o, h, ln_mix_g[i][None, :], ln_mix_b[i][None, :], alpha)
        h = _ffn(h, ffn_w_gu[i].astype(BF16), ffn_w_down[i].astype(BF16),
                 ln_ffn_g[i][None, :], ln_ffn_b[i][None, :], alpha)
    return h.reshape(batch, seq, d)
```

```python
import functools

import jax
import jax.numpy as jnp
import numpy as np
from jax import lax
from jax.experimental import pallas as pl
from jax.experimental.pallas import tpu as pltpu

CHUNK = 64
MLA_HEADS = 16
MLA_NOPE_DIM = 128
MLA_ROPE_DIM = 64
MLA_V_DIM = 128
MLA_Q_RANK = 512
MLA_KV_RANK = 512
ROPE_THETA = 10000.0
FOX_HEADS = 16
FOX_HEAD_DIM = 128
LN_EPS = 1e-5
RMS_EPS = 1e-6
NEG_INF = -1e30
LOG2E = 1.4426950408889634

LANES = 128
VMEM_LIMIT_BYTES = 56 * 1024 * 1024
ATTN_HEADS_PER_STEP = 4
ATTN_TILES_PER_BLOCK = 4

BF16 = jnp.bfloat16
F32 = jnp.float32


def _params(*semantics):
    return pltpu.CompilerParams(dimension_semantics=semantics, vmem_limit_bytes=VMEM_LIMIT_BYTES)


def _dot(a, b):
    return jnp.dot(a, b, preferred_element_type=F32)


def _layer_norm(y, g, b):
    mu = jnp.mean(y, axis=-1, keepdims=True)
    d = y - mu
    var = jnp.mean(d * d, axis=-1, keepdims=True)
    return d * lax.rsqrt(var + LN_EPS) * g + b


def _rms_norm(y, g):
    return y * lax.rsqrt(jnp.mean(y * y, axis=-1, keepdims=True) + RMS_EPS) * g


def _mla_in_kernel(x_ref, pos_ref, freq_ref, w_ref, gq_ref, gkv_ref,
                   cq_ref, ckv_ref, kr_ref, cos_ref, sin_ref):
    h = _dot(x_ref[...].astype(BF16), w_ref[...])
    cq_ref[...] = _rms_norm(h[:, :MLA_Q_RANK], gq_ref[...]).astype(BF16)
    ckv_ref[...] = _rms_norm(h[:, MLA_Q_RANK:MLA_Q_RANK + MLA_KV_RANK], gkv_ref[...]).astype(BF16)
    ang = pos_ref[...] * freq_ref[...]
    c = jnp.cos(ang)
    s = jnp.sin(ang)
    base = MLA_Q_RANK + MLA_KV_RANK
    kr_ref[...] = (h[:, base:base + LANES] * c + h[:, base + LANES:] * s).astype(BF16)
    cos_ref[...] = c
    sin_ref[...] = s


def _mla_in(x, pos, freq, w, gq, gkv, tm=512):
    n, d = x.shape
    row = lambda i: (i, 0)
    fixed = lambda i: (0, 0)
    return pl.pallas_call(
        _mla_in_kernel,
        grid=(n // tm,),
        in_specs=[pl.BlockSpec((tm, d), row), pl.BlockSpec((tm, 1), row),
                  pl.BlockSpec((1, LANES), fixed), pl.BlockSpec(w.shape, fixed),
                  pl.BlockSpec((1, MLA_Q_RANK), fixed), pl.BlockSpec((1, MLA_KV_RANK), fixed)],
        out_specs=[pl.BlockSpec((tm, MLA_Q_RANK), row), pl.BlockSpec((tm, MLA_KV_RANK), row),
                   pl.BlockSpec((tm, LANES), row), pl.BlockSpec((tm, LANES), row),
                   pl.BlockSpec((tm, LANES), row)],
        out_shape=[jax.ShapeDtypeStruct((n, MLA_Q_RANK), BF16),
                   jax.ShapeDtypeStruct((n, MLA_KV_RANK), BF16),
                   jax.ShapeDtypeStruct((n, LANES), BF16),
                   jax.ShapeDtypeStruct((n, LANES), F32),
                   jax.ShapeDtypeStruct((n, LANES), F32)],
        compiler_params=_params("arbitrary"),
        name="mla_in",
    )(x, pos, freq, w, gq, gkv)


def _mla_q_kernel(cq_ref, cos_ref, sin_ref, w_ref, qn_ref, qr_ref, *, scale):
    n_nope = MLA_HEADS * MLA_NOPE_DIM
    n_rope = MLA_HEADS * MLA_ROPE_DIM
    cq = cq_ref[...]
    qn_ref[...] = (_dot(cq, w_ref[:, :n_nope]) * scale).astype(BF16)
    r = _dot(cq, w_ref[:, n_nope:n_nope + n_rope])
    rp = _dot(cq, w_ref[:, n_nope + n_rope:])
    c = cos_ref[...]
    s = sin_ref[...]
    for p in range(n_rope // LANES):
        sl = slice(p * LANES, (p + 1) * LANES)
        qr_ref[:, sl] = ((r[:, sl] * c + rp[:, sl] * s) * scale).astype(BF16)


def _mla_q(cq, cos, sin, w, scale, tm=512):
    n = cq.shape[0]
    n_nope = MLA_HEADS * MLA_NOPE_DIM
    n_rope = MLA_HEADS * MLA_ROPE_DIM
    row = lambda i: (i, 0)
    fixed = lambda i: (0, 0)
    return pl.pallas_call(
        functools.partial(_mla_q_kernel, scale=scale),
        grid=(n // tm,),
        in_specs=[pl.BlockSpec((tm, MLA_Q_RANK), row), pl.BlockSpec((tm, LANES), row),
                  pl.BlockSpec((tm, LANES), row), pl.BlockSpec(w.shape, fixed)],
        out_specs=[pl.BlockSpec((tm, n_nope), row), pl.BlockSpec((tm, n_rope), row)],
        out_shape=[jax.ShapeDtypeStruct((n, n_nope), BF16), jax.ShapeDtypeStruct((n, n_rope), BF16)],
        compiler_params=_params("arbitrary"),
        name="mla_q_up",
    )(cq, cos, sin, w)


def _matmul_kernel(a_ref, b_ref, o_ref, a_bf, *, n_scaled, scale):
    j = pl.program_id(1)

    @pl.when(j == 0)
    def _():
        a_bf[...] = a_ref[...].astype(BF16)

    acc = _dot(a_bf[...], b_ref[...])
    if n_scaled:
        acc = acc * jnp.where(j < n_scaled, scale, 1.0).astype(F32)
    o_ref[...] = acc.astype(o_ref.dtype)


def _matmul(a, b, n, tm, tn, n_scaled=0, scale=1.0):
    m, k = a.shape
    return pl.pallas_call(
        functools.partial(_matmul_kernel, n_scaled=n_scaled, scale=scale),
        grid=(m // tm, n // tn),
        in_specs=[pl.BlockSpec((tm, k), lambda i, j: (i, 0)), pl.BlockSpec((k, tn), lambda i, j: (0, j))],
        out_specs=pl.BlockSpec((tm, tn), lambda i, j: (i, j)),
        out_shape=jax.ShapeDtypeStruct((m, n), BF16),
        scratch_shapes=[pltpu.VMEM((tm, k), BF16)],
        compiler_params=_params("arbitrary", "arbitrary"),
        name="matmul",
    )(a, b)


def _split3(v):
    hi = v.astype(BF16).astype(F32)
    r1 = v - hi
    mid = r1.astype(BF16).astype(F32)
    lo = (r1 - mid).astype(BF16).astype(F32)
    return hi, mid, lo


def _fox_gate_kernel(x_ref, w_ref, b_ref, tri_ref, aq_ref, ak_ref, carry):
    t = pl.program_id(1)

    @pl.when(t == 0)
    def _():
        carry[...] = jnp.zeros_like(carry)

    logit = _dot(x_ref[...].astype(BF16), w_ref[...]) + b_ref[...]
    log_f = jnp.minimum(logit, 0.0) - jnp.log1p(jnp.exp(-jnp.abs(logit)))
    tri = tri_ref[...]
    hi, mid, lo = _split3(log_f)
    c = (_dot(tri, hi.astype(BF16)) + _dot(tri, mid.astype(BF16))) + _dot(tri, lo.astype(BF16))
    c = c + carry[0:1, :]
    ts = c.shape[0]
    carry[0:1, :] = c[ts - 1:ts, :]
    hi, mid, lo = _split3(c * LOG2E)
    group = lax.broadcasted_iota(jnp.int32, c.shape, 1) // FOX_HEADS
    pieces = jnp.where(group % 3 == 0, hi, jnp.where(group % 3 == 1, mid, lo))
    aq_ref[...] = jnp.where(group < 3, pieces, jnp.where(group < 6, 1.0, 0.0)).astype(BF16)
    ak_ref[...] = jnp.where(group < 3, 1.0, jnp.where(group < 6, -pieces, 0.0)).astype(BF16)


def _fox_gate(x, w, b, batch, ts=512):
    n, d = x.shape
    steps = n // batch // ts
    tri = jnp.asarray(np.tril(np.ones((ts, ts), np.float32)), BF16)
    row = lambda bi, t: (bi * steps + t, 0)
    fixed = lambda bi, t: (0, 0)
    return pl.pallas_call(
        _fox_gate_kernel,
        grid=(batch, steps),
        in_specs=[pl.BlockSpec((ts, d), row), pl.BlockSpec(w.shape, fixed),
                  pl.BlockSpec((1, LANES), fixed), pl.BlockSpec((ts, ts), fixed)],
        out_specs=[pl.BlockSpec((ts, LANES), row), pl.BlockSpec((ts, LANES), row)],
        out_shape=[jax.ShapeDtypeStruct((n, LANES), BF16), jax.ShapeDtypeStruct((n, LANES), BF16)],
        scratch_shapes=[pltpu.VMEM((8, LANES), F32)],
        compiler_params=_params("arbitrary", "arbitrary"),
        name="fox_gate",
    )(x, w, b, tri)


def _attn_kernel(qm_ref, qa_ref, lm_ref, km_ref, ka_ref, v_ref, bias_ref, o_ref,
                 kcat, vcat, m_sc, acc_sc, *, tq, hpb):
    i = pl.program_id(2)

    @pl.when(i == 0)
    def _():
        ones = jnp.ones(ka_ref.shape, BF16)
        for g in range(hpb):
            head = slice(g * LANES, (g + 1) * LANES)
            kcat[g, :, :LANES] = km_ref[:, head]
            kcat[g, :, LANES:] = ka_ref[...]
            vcat[g, :, :LANES] = v_ref[:, head]
            vcat[g, :, LANES:] = ones

    qa_tiles = qa_ref.shape[1] // LANES

    def q_tile(g):
        t = g * qa_tiles // hpb
        aux = qa_ref[:, t * LANES:(t + 1) * LANES] * lm_ref[g, 0:1, :]
        return jnp.concatenate([qm_ref[:, g * LANES:(g + 1) * LANES], aux], axis=1)

    q = [q_tile(g) for g in range(hpb)]

    def tile(g, j, bias):
        rows = pl.ds(pl.multiple_of(j * tq, tq), tq)
        s = lax.dot_general(q[g], kcat[g, rows, :], (((1,), (1,)), ((), ())), preferred_element_type=F32)
        if bias is not None:
            s = s + bias
        mx = jnp.broadcast_to(jnp.max(s, axis=-1, keepdims=True), (tq, LANES))
        m_new = jnp.maximum(m_sc[g], mx)
        p = jnp.exp2(s - jnp.tile(m_new, (1, tq // LANES))).astype(BF16)
        pv = _dot(p, vcat[g, rows, :])
        acc_sc[g] = jnp.tile(jnp.exp2(m_sc[g] - m_new), (1, 2)) * acc_sc[g] + pv
        m_sc[g] = m_new

    def tiles(js_and_biases):
        for j, bias in js_and_biases:
            for g in range(hpb):
                tile(g, j, bias)

    m_sc[...] = jnp.full(m_sc.shape, -jnp.inf, F32)
    acc_sc[...] = jnp.zeros_like(acc_sc)
    group = ATTN_TILES_PER_BLOCK
    rem = i % group
    for r in range(group):
        @pl.when(rem == r)
        def _(r=r):
            tiles([(i, bias_ref[...])] + [(i - t, None) for t in range(1, r + 1)])

    def body(jj, carry):
        tiles([(group * jj + t, None) for t in range(group)])
        return carry

    lax.fori_loop(0, i // group, body, 0)
    for g in range(hpb):
        acc = acc_sc[g]
        o_ref[:, g * LANES:(g + 1) * LANES] = (acc[:, :LANES] / acc[:, LANES:]).astype(o_ref.dtype)


def _attention(qm, qm_col0, qa, qa_tiles, lane_mask, km, km_col0, ka, v, v_col0, bias, batch, heads,
               tq=512, hpb=ATTN_HEADS_PER_STEP):
    n = qm.shape[0]
    seq = n // batch
    nq = seq // tq
    wide = hpb * LANES
    q_row = lambda b, i: b * nq + i
    return pl.pallas_call(
        functools.partial(_attn_kernel, tq=tq, hpb=hpb),
        grid=(batch, heads // hpb, nq),
        in_specs=[
            pl.BlockSpec((tq, wide), lambda b, h, i: (q_row(b, i), qm_col0 // hpb + h)),
            pl.BlockSpec((tq, max(qa_tiles, 1) * LANES), lambda b, h, i: (q_row(b, i), h if qa_tiles else 0)),
            pl.BlockSpec((hpb, 8, LANES), lambda b, h, i: (h, 0, 0)),
            pl.BlockSpec((seq, wide), lambda b, h, i: (b, km_col0 // hpb + h)),
            pl.BlockSpec((seq, LANES), lambda b, h, i: (b, 0)),
            pl.BlockSpec((seq, wide), lambda b, h, i: (b, v_col0 // hpb + h)),
            pl.BlockSpec((tq, tq), lambda b, h, i: (0, 0)),
        ],
        out_specs=pl.BlockSpec((tq, wide), lambda b, h, i: (q_row(b, i), h)),
        out_shape=jax.ShapeDtypeStruct((n, heads * LANES), BF16),
        scratch_shapes=[pltpu.VMEM((hpb, seq, 2 * LANES), BF16), pltpu.VMEM((hpb, seq, 2 * LANES), BF16),
                        pltpu.VMEM((hpb, tq, LANES), F32), pltpu.VMEM((hpb, tq, 2 * LANES), F32)],
        compiler_params=_params("arbitrary", "arbitrary", "arbitrary"),
        name="attention",
    )(qm, qa, lane_mask, km, ka, v, bias)


def _proj_ln_kernel(o_ref, w_ref, x_ref, g_ref, b_ref, y_ref, *, alpha):
    y = alpha * x_ref[...] + _dot(o_ref[...], w_ref[...])
    y_ref[...] = _layer_norm(y, g_ref[...], b_ref[...])


def _proj_ln(o, w, x, g, b, alpha, tm=512):
    n, d = x.shape
    row = lambda i: (i, 0)
    fixed = lambda i: (0, 0)
    return pl.pallas_call(
        functools.partial(_proj_ln_kernel, alpha=alpha),
        grid=(n // tm,),
        in_specs=[pl.BlockSpec((tm, o.shape[1]), row), pl.BlockSpec(w.shape, fixed),
                  pl.BlockSpec((tm, d), row), pl.BlockSpec((1, d), fixed), pl.BlockSpec((1, d), fixed)],
        out_specs=pl.BlockSpec((tm, d), row),
        out_shape=jax.ShapeDtypeStruct((n, d), F32),
        compiler_params=_params("arbitrary"),
        name="proj_ln",
    )(o, w, x, g, b)


def _ffn_kernel(x_ref, wg_ref, wu_ref, wd_ref, g_ref, b_ref, y_ref, x_bf, *, alpha):
    j = pl.program_id(1)

    @pl.when(j == 0)
    def _():
        x_bf[...] = x_ref[...].astype(BF16)
        y_ref[...] = jnp.zeros_like(y_ref)

    xb = x_bf[...]
    gate = _dot(xb, wg_ref[...])
    up = _dot(xb, wu_ref[...])
    y_ref[...] += _dot((gate * jax.nn.sigmoid(gate) * up).astype(BF16), wd_ref[...])

    @pl.when(j == pl.num_programs(1) - 1)
    def _():
        y_ref[...] = _layer_norm(alpha * x_ref[...] + y_ref[...], g_ref[...], b_ref[...])


def _ffn(x, w_gu, w_down, layer, g, b, alpha, tm=512, tf=512):
    n, d = x.shape
    hidden = w_down.shape[1]
    nf = hidden // tf
    return pl.pallas_call(
        functools.partial(_ffn_kernel, alpha=alpha),
        grid=(n // tm, nf),
        in_specs=[pl.BlockSpec((tm, d), lambda i, j: (i, 0)),
                  pl.BlockSpec((None, d, tf), lambda i, j: (layer, 0, j)),
                  pl.BlockSpec((None, d, tf), lambda i, j: (layer, 0, nf + j)),
                  pl.BlockSpec((None, tf, d), lambda i, j: (layer, j, 0)),
                  pl.BlockSpec((1, d), lambda i, j: (0, 0)), pl.BlockSpec((1, d), lambda i, j: (0, 0))],
        out_specs=pl.BlockSpec((tm, d), lambda i, j: (i, 0)),
        out_shape=jax.ShapeDtypeStruct((n, d), F32),
        scratch_shapes=[pltpu.VMEM((tm, d), BF16)],
        compiler_params=_params("arbitrary", "arbitrary"),
        name="ffn",
    )(x, w_gu, w_gu, w_down, g, b)


def _mla_weights(w_in, w_q_up, w_kv_up):
    d = w_in.shape[0]
    half = MLA_ROPE_DIM // 2
    base = MLA_Q_RANK + MLA_KV_RANK
    k1, k2 = w_in[:, base:base + half], w_in[:, base + half:]
    w_in_cat = jnp.concatenate([w_in[:, :base], k1, k2, k1, k2, -k2, k1, -k2, k1], axis=1)
    wq = w_q_up.reshape(MLA_Q_RANK, MLA_HEADS, MLA_NOPE_DIM + MLA_ROPE_DIM)
    r1 = wq[:, :, MLA_NOPE_DIM:MLA_NOPE_DIM + half]
    r2 = wq[:, :, MLA_NOPE_DIM + half:]
    w_q_cat = jnp.concatenate([
        wq[:, :, :MLA_NOPE_DIM].reshape(MLA_Q_RANK, -1),
        wq[:, :, MLA_NOPE_DIM:].reshape(MLA_Q_RANK, -1),
        jnp.concatenate([-r2, r1], axis=2).reshape(MLA_Q_RANK, -1)], axis=1)
    wkv = w_kv_up.reshape(MLA_KV_RANK, MLA_HEADS, MLA_NOPE_DIM + MLA_V_DIM)
    w_kv_cat = jnp.concatenate([wkv[:, :, :MLA_NOPE_DIM].reshape(MLA_KV_RANK, -1),
                                wkv[:, :, MLA_NOPE_DIM:].reshape(MLA_KV_RANK, -1)], axis=1)
    del d
    return w_in_cat.astype(BF16), w_q_cat.astype(BF16), w_kv_cat.astype(BF16)


def _causal_bias(tq, granularity):
    r = np.arange(tq)[:, None] // granularity
    c = np.arange(tq)[None, :] // granularity
    return jnp.asarray(np.where(c <= r, 0.0, NEG_INF), F32)


def _mla_lane_mask():
    lane = np.arange(LANES)[None, :]
    head = np.arange(MLA_HEADS)[:, None]
    m = (lane // MLA_ROPE_DIM == head % 2).astype(np.float32)
    return jnp.asarray(np.broadcast_to(m[:, None, :], (MLA_HEADS, 8, LANES)), BF16)


def _fox_lane_mask():
    lane = np.arange(LANES)[None, :]
    head = np.arange(FOX_HEADS)[:, None]
    m = ((lane % FOX_HEADS == head) & (lane < 6 * FOX_HEADS)).astype(np.float32)
    return jnp.asarray(np.broadcast_to(m[:, None, :], (FOX_HEADS, 8, LANES)), BF16)


def _mla_layer(x, pos, w_in, q_norm_g, w_q_up, kv_norm_g, w_kv_up, w_o, batch):
    w_in_cat, w_q_cat, w_kv_cat = _mla_weights(w_in, w_q_up, w_kv_up)
    half = MLA_ROPE_DIM // 2
    inv_freq = ROPE_THETA ** (-jnp.arange(0, MLA_ROPE_DIM, 2, dtype=F32) / MLA_ROPE_DIM)
    freq = jnp.tile(inv_freq, LANES // half)[None, :]
    cq, ckv, kr, cos, sin = _mla_in(x, pos, freq, w_in_cat, q_norm_g[None, :], kv_norm_g[None, :])
    scale = (MLA_NOPE_DIM + MLA_ROPE_DIM) ** -0.5 * LOG2E
    qn, qr = _mla_q(cq, cos, sin, w_q_cat, scale)
    kv = _matmul(ckv, w_kv_cat, w_kv_cat.shape[1], tm=1024, tn=1024)
    o = _attention(qn, 0, qr, ATTN_HEADS_PER_STEP // 2, _mla_lane_mask(), kv, 0, kr, kv, MLA_HEADS,
                   _causal_bias(512, CHUNK), batch, MLA_HEADS)
    return o, w_o.astype(BF16)


def _fox_layer(x, w_in, b_f, w_o, batch):
    width = FOX_HEADS * FOX_HEAD_DIM
    w_bf = w_in.astype(BF16)
    qkv = _matmul(x, w_bf, 3 * width, tm=1024, tn=1024,
                  n_scaled=width // 1024, scale=FOX_HEAD_DIM ** -0.5 * LOG2E)
    groups = 6
    w_f = jnp.pad(jnp.tile(w_bf[:, 3 * width:], (1, groups)), ((0, 0), (0, LANES - groups * FOX_HEADS)))
    b6 = jnp.pad(jnp.tile(b_f, groups), (0, LANES - groups * FOX_HEADS))[None, :]
    aq, ak = _fox_gate(x, w_f, b6, batch)
    o = _attention(qkv, 0, aq, 0, _fox_lane_mask(), qkv, FOX_HEADS, ak, qkv, 2 * FOX_HEADS,
                   _causal_bias(512, 1), batch, FOX_HEADS)
    return o, w_o.astype(BF16)


def kernel(x, positions, mla_w_in, mla_q_norm_g, mla_w_q_up, mla_kv_norm_g, mla_w_kv_up, mla_w_o,
           fox_w_in, fox_b_f, fox_w_o, ffn_w_gu, ffn_w_down, ln_mix_g, ln_mix_b, ln_ffn_g, ln_ffn_b):
    batch, seq, d = x.shape
    depth = ffn_w_gu.shape[0]
    alpha = float((2 * depth) ** 0.25)
    h = x.reshape(batch * seq, d)
    pos = positions.astype(F32).reshape(batch * seq, 1)
    w_gu_bf = ffn_w_gu.astype(BF16)
    w_down_bf = ffn_w_down.astype(BF16)
    for i in range(depth):
        j = i // 2
        if i % 2 == 0:
            o, w_o = _mla_layer(h, pos, mla_w_in[j], mla_q_norm_g[j], mla_w_q_up[j], mla_kv_norm_g[j],
                                mla_w_kv_up[j], mla_w_o[j], batch)
        else:
            o, w_o = _fox_layer(h, fox_w_in[j], fox_b_f[j], fox_w_o[j], batch)
        h = _proj_ln(o, w_o, h, ln_mix_g[i][None, :], ln_mix_b[i][None, :], alpha)
        h = _ffn(h, w_gu_bf, w_down_bf, i, ln_ffn_g[i][None, :], ln_ffn_b[i][None, :], alpha)
    return h.reshape(batch, seq, d)
```

```python
import functools

import jax
import jax.numpy as jnp
import numpy as np
from jax import lax
from jax.experimental import pallas as pl
from jax.experimental.pallas import tpu as pltpu

CHUNK = 64
MLA_HEADS = 16
MLA_NOPE_DIM = 128
MLA_ROPE_DIM = 64
MLA_V_DIM = 128
MLA_Q_RANK = 512
MLA_KV_RANK = 512
ROPE_THETA = 10000.0
FOX_HEADS = 16
FOX_HEAD_DIM = 128
LN_EPS = 1e-5
RMS_EPS = 1e-6
NEG_INF = -1e30
LOG2E = 1.4426950408889634

LANES = 128
VMEM_LIMIT_BYTES = 62 * 1024 * 1024
ATTN_HEADS_PER_STEP = 4
ATTN_TILES_PER_BLOCK = 4
FFN_ROW_CHUNK = 512
LN_ROW_CHUNK = 256

BF16 = jnp.bfloat16
F32 = jnp.float32


def _params(*semantics):
    return pltpu.CompilerParams(dimension_semantics=semantics, vmem_limit_bytes=VMEM_LIMIT_BYTES)


def _dot(a, b):
    return jnp.dot(a, b, preferred_element_type=F32)


def _layer_norm(y, g, b):
    mu = jnp.mean(y, axis=-1, keepdims=True)
    d = y - mu
    var = jnp.mean(d * d, axis=-1, keepdims=True)
    return d * lax.rsqrt(var + LN_EPS) * g + b


def _rms_norm(y, g):
    return y * lax.rsqrt(jnp.mean(y * y, axis=-1, keepdims=True) + RMS_EPS) * g


def _mla_in_kernel(x_ref, pos_ref, freq_ref, w_ref, gq_ref, gkv_ref,
                   cq_ref, ckv_ref, kr_ref, cos_ref, sin_ref):
    h = _dot(x_ref[...].astype(BF16), w_ref[...])
    cq_ref[...] = _rms_norm(h[:, :MLA_Q_RANK], gq_ref[...]).astype(BF16)
    ckv_ref[...] = _rms_norm(h[:, MLA_Q_RANK:MLA_Q_RANK + MLA_KV_RANK], gkv_ref[...]).astype(BF16)
    ang = pos_ref[...] * freq_ref[...]
    c = jnp.cos(ang)
    s = jnp.sin(ang)
    base = MLA_Q_RANK + MLA_KV_RANK
    kr_ref[...] = (h[:, base:base + LANES] * c + h[:, base + LANES:] * s).astype(BF16)
    cos_ref[...] = c
    sin_ref[...] = s


def _mla_in(x, pos, freq, w, gq, gkv, tm=512):
    n, d = x.shape
    row = lambda i: (i, 0)
    fixed = lambda i: (0, 0)
    return pl.pallas_call(
        _mla_in_kernel,
        grid=(n // tm,),
        in_specs=[pl.BlockSpec((tm, d), row), pl.BlockSpec((tm, 1), row),
                  pl.BlockSpec((1, LANES), fixed), pl.BlockSpec(w.shape, fixed),
                  pl.BlockSpec((1, MLA_Q_RANK), fixed), pl.BlockSpec((1, MLA_KV_RANK), fixed)],
        out_specs=[pl.BlockSpec((tm, MLA_Q_RANK), row), pl.BlockSpec((tm, MLA_KV_RANK), row),
                   pl.BlockSpec((tm, LANES), row), pl.BlockSpec((tm, LANES), row),
                   pl.BlockSpec((tm, LANES), row)],
        out_shape=[jax.ShapeDtypeStruct((n, MLA_Q_RANK), BF16),
                   jax.ShapeDtypeStruct((n, MLA_KV_RANK), BF16),
                   jax.ShapeDtypeStruct((n, LANES), BF16),
                   jax.ShapeDtypeStruct((n, LANES), F32),
                   jax.ShapeDtypeStruct((n, LANES), F32)],
        compiler_params=_params("arbitrary"),
        name="mla_in",
    )(x, pos, freq, w, gq, gkv)


def _mla_q_kernel(cq_ref, cos_ref, sin_ref, w_ref, qn_ref, qr_ref, *, scale):
    n_nope = MLA_HEADS * MLA_NOPE_DIM
    n_rope = MLA_HEADS * MLA_ROPE_DIM
    cq = cq_ref[...]
    qn_ref[...] = (_dot(cq, w_ref[:, :n_nope]) * scale).astype(BF16)
    r = _dot(cq, w_ref[:, n_nope:n_nope + n_rope])
    rp = _dot(cq, w_ref[:, n_nope + n_rope:])
    c = cos_ref[...]
    s = sin_ref[...]
    for p in range(n_rope // LANES):
        sl = slice(p * LANES, (p + 1) * LANES)
        qr_ref[:, sl] = ((r[:, sl] * c + rp[:, sl] * s) * scale).astype(BF16)


def _mla_q(cq, cos, sin, w, scale, tm=512):
    n = cq.shape[0]
    n_nope = MLA_HEADS * MLA_NOPE_DIM
    n_rope = MLA_HEADS * MLA_ROPE_DIM
    row = lambda i: (i, 0)
    fixed = lambda i: (0, 0)
    return pl.pallas_call(
        functools.partial(_mla_q_kernel, scale=scale),
        grid=(n // tm,),
        in_specs=[pl.BlockSpec((tm, MLA_Q_RANK), row), pl.BlockSpec((tm, LANES), row),
                  pl.BlockSpec((tm, LANES), row), pl.BlockSpec(w.shape, fixed)],
        out_specs=[pl.BlockSpec((tm, n_nope), row), pl.BlockSpec((tm, n_rope), row)],
        out_shape=[jax.ShapeDtypeStruct((n, n_nope), BF16), jax.ShapeDtypeStruct((n, n_rope), BF16)],
        compiler_params=_params("arbitrary"),
        name="mla_q_up",
    )(cq, cos, sin, w)


def _matmul_kernel(a_ref, b_ref, o_ref, a_bf, *, n_scaled, scale):
    j = pl.program_id(1)

    @pl.when(j == 0)
    def _():
        a_bf[...] = a_ref[...].astype(BF16)

    acc = _dot(a_bf[...], b_ref[...])
    if n_scaled:
        acc = acc * jnp.where(j < n_scaled, scale, 1.0).astype(F32)
    o_ref[...] = acc.astype(o_ref.dtype)


def _matmul(a, b, n, tm, tn, n_scaled=0, scale=1.0):
    m, k = a.shape
    return pl.pallas_call(
        functools.partial(_matmul_kernel, n_scaled=n_scaled, scale=scale),
        grid=(m // tm, n // tn),
        in_specs=[pl.BlockSpec((tm, k), lambda i, j: (i, 0)), pl.BlockSpec((k, tn), lambda i, j: (0, j))],
        out_specs=pl.BlockSpec((tm, tn), lambda i, j: (i, j)),
        out_shape=jax.ShapeDtypeStruct((m, n), BF16),
        scratch_shapes=[pltpu.VMEM((tm, k), BF16)],
        compiler_params=_params("arbitrary", "arbitrary"),
        name="matmul",
    )(a, b)


def _split3(v):
    hi = v.astype(BF16).astype(F32)
    r1 = v - hi
    mid = r1.astype(BF16).astype(F32)
    lo = (r1 - mid).astype(BF16).astype(F32)
    return hi, mid, lo


def _fox_gate_kernel(x_ref, w_ref, b_ref, tri_ref, aq_ref, ak_ref, carry):
    t = pl.program_id(1)

    @pl.when(t == 0)
    def _():
        carry[...] = jnp.zeros_like(carry)

    logit = _dot(x_ref[...].astype(BF16), w_ref[...]) + b_ref[...]
    log_f = jnp.minimum(logit, 0.0) - jnp.log1p(jnp.exp(-jnp.abs(logit)))
    tri = tri_ref[...]
    hi, mid, lo = _split3(log_f)
    c = (_dot(tri, hi.astype(BF16)) + _dot(tri, mid.astype(BF16))) + _dot(tri, lo.astype(BF16))
    c = c + carry[0:1, :]
    ts = c.shape[0]
    carry[0:1, :] = c[ts - 1:ts, :]
    hi, mid, lo = _split3(c * LOG2E)
    group = lax.broadcasted_iota(jnp.int32, c.shape, 1) // FOX_HEADS
    pieces = jnp.where(group % 3 == 0, hi, jnp.where(group % 3 == 1, mid, lo))
    aq_ref[...] = jnp.where(group < 3, pieces, jnp.where(group < 6, 1.0, 0.0)).astype(BF16)
    ak_ref[...] = jnp.where(group < 3, 1.0, jnp.where(group < 6, -pieces, 0.0)).astype(BF16)


def _fox_gate(x, w, b, batch, ts=512):
    n, d = x.shape
    steps = n // batch // ts
    tri = jnp.asarray(np.tril(np.ones((ts, ts), np.float32)), BF16)
    row = lambda bi, t: (bi * steps + t, 0)
    fixed = lambda bi, t: (0, 0)
    return pl.pallas_call(
        _fox_gate_kernel,
        grid=(batch, steps),
        in_specs=[pl.BlockSpec((ts, d), row), pl.BlockSpec(w.shape, fixed),
                  pl.BlockSpec((1, LANES), fixed), pl.BlockSpec((ts, ts), fixed)],
        out_specs=[pl.BlockSpec((ts, LANES), row), pl.BlockSpec((ts, LANES), row)],
        out_shape=[jax.ShapeDtypeStruct((n, LANES), BF16), jax.ShapeDtypeStruct((n, LANES), BF16)],
        scratch_shapes=[pltpu.VMEM((8, LANES), F32)],
        compiler_params=_params("arbitrary", "arbitrary"),
        name="fox_gate",
    )(x, w, b, tri)


def _attn_kernel(qm_ref, qa_ref, lm_ref, km_ref, ka_ref, v_ref, bias_ref, *rest, tq, hpb, n_cast):
    cast_in, (o_ref, *cast_out), (kcat, vcat, m_sc, acc_sc) = rest[:n_cast], rest[n_cast:2 * n_cast + 1], rest[2 * n_cast + 1:]
    i = pl.program_id(2)
    for src, dst in zip(cast_in, cast_out):
        dst[...] = src[...].astype(BF16)

    @pl.when(i == 0)
    def _():
        ones = jnp.ones(ka_ref.shape, BF16)
        for g in range(hpb):
            head = slice(g * LANES, (g + 1) * LANES)
            kcat[g, :, :LANES] = km_ref[:, head]
            kcat[g, :, LANES:] = ka_ref[...]
            vcat[g, :, :LANES] = v_ref[:, head]
            vcat[g, :, LANES:] = ones

    qa_tiles = qa_ref.shape[1] // LANES

    def q_tile(g):
        t = g * qa_tiles // hpb
        aux = qa_ref[:, t * LANES:(t + 1) * LANES] * lm_ref[g, 0:1, :]
        return jnp.concatenate([qm_ref[:, g * LANES:(g + 1) * LANES], aux], axis=1)

    q = [q_tile(g) for g in range(hpb)]

    def tile(g, j, bias):
        rows = pl.ds(pl.multiple_of(j * tq, tq), tq)
        s = lax.dot_general(q[g], kcat[g, rows, :], (((1,), (1,)), ((), ())), preferred_element_type=F32)
        diagonal = bias is not None
        if diagonal:
            s = s + bias
        mx = jnp.broadcast_to(jnp.max(s, axis=-1, keepdims=True), (tq, LANES))
        m_new = mx if diagonal else jnp.maximum(m_sc[g], mx)
        p = jnp.exp2(s - jnp.tile(m_new, (1, tq // LANES))).astype(BF16)
        pv = _dot(p, vcat[g, rows, :])
        if diagonal:
            acc_sc[g] = pv
        else:
            acc_sc[g] = jnp.tile(jnp.exp2(m_sc[g] - m_new), (1, 2)) * acc_sc[g] + pv
        m_sc[g] = m_new

    def tiles(js_and_biases):
        for j, bias in js_and_biases:
            for g in range(hpb):
                tile(g, j, bias)

    group = ATTN_TILES_PER_BLOCK
    rem = i % group
    for r in range(group):
        @pl.when(rem == r)
        def _(r=r):
            tiles([(i, bias_ref[...])] + [(i - t, None) for t in range(1, r + 1)])

    def body(jj, carry):
        tiles([(group * jj + t, None) for t in range(group)])
        return carry

    lax.fori_loop(0, i // group, body, 0)
    for g in range(hpb):
        acc = acc_sc[g]
        o_ref[:, g * LANES:(g + 1) * LANES] = (acc[:, :LANES] / acc[:, LANES:]).astype(o_ref.dtype)


def _attention(qm, qm_col0, qa, qa_tiles, lane_mask, km, km_col0, ka, v, v_col0, bias, batch, heads,
               cast_jobs=(), tq=512, hpb=ATTN_HEADS_PER_STEP):
    n = qm.shape[0]
    seq = n // batch
    nq = seq // tq
    wide = hpb * LANES
    head_blocks = heads // hpb
    steps = batch * head_blocks * nq
    q_row = lambda b, i: b * nq + i

    def cast_spec(arr):
        blocks = _cast_blocks(arr.shape[0], steps)
        rows, repeat = arr.shape[0] // blocks, steps // blocks
        return pl.BlockSpec((rows, arr.shape[1]), lambda b, h, i: (((b * head_blocks + h) * nq + i) // repeat, 0))

    cast_specs = [cast_spec(arr) for arr in cast_jobs]
    out, *copies = pl.pallas_call(
        functools.partial(_attn_kernel, tq=tq, hpb=hpb, n_cast=len(cast_jobs)),
        grid=(batch, head_blocks, nq),
        in_specs=[
            pl.BlockSpec((tq, wide), lambda b, h, i: (q_row(b, i), qm_col0 // hpb + h)),
            pl.BlockSpec((tq, max(qa_tiles, 1) * LANES), lambda b, h, i: (q_row(b, i), h if qa_tiles else 0)),
            pl.BlockSpec((hpb, 8, LANES), lambda b, h, i: (h, 0, 0)),
            pl.BlockSpec((seq, wide), lambda b, h, i: (b, km_col0 // hpb + h)),
            pl.BlockSpec((seq, LANES), lambda b, h, i: (b, 0)),
            pl.BlockSpec((seq, wide), lambda b, h, i: (b, v_col0 // hpb + h)),
            pl.BlockSpec((tq, tq), lambda b, h, i: (0, 0)),
        ] + cast_specs,
        out_specs=[pl.BlockSpec((tq, wide), lambda b, h, i: (q_row(b, i), h))] + cast_specs,
        out_shape=[jax.ShapeDtypeStruct((n, heads * LANES), BF16)]
        + [jax.ShapeDtypeStruct(arr.shape, BF16) for arr in cast_jobs],
        scratch_shapes=[pltpu.VMEM((hpb, seq, 2 * LANES), BF16), pltpu.VMEM((hpb, seq, 2 * LANES), BF16),
                        pltpu.VMEM((hpb, tq, LANES), F32), pltpu.VMEM((hpb, tq, 2 * LANES), F32)],
        compiler_params=_params("arbitrary", "arbitrary", "arbitrary"),
        name="attention",
    )(qm, qa, lane_mask, km, ka, v, bias, *cast_jobs)
    return out, copies


def _proj_ln_kernel(o_ref, w_ref, x_ref, g_ref, b_ref, y_ref, *, alpha):
    y = alpha * x_ref[...] + _dot(o_ref[...], w_ref[...])
    y_ref[...] = _layer_norm(y, g_ref[...], b_ref[...])


def _proj_ln(o, w, x, g, b, alpha, tm=512):
    n, d = x.shape
    row = lambda i: (i, 0)
    fixed = lambda i: (0, 0)
    return pl.pallas_call(
        functools.partial(_proj_ln_kernel, alpha=alpha),
        grid=(n // tm,),
        in_specs=[pl.BlockSpec((tm, o.shape[1]), row), pl.BlockSpec(w.shape, fixed),
                  pl.BlockSpec((tm, d), row), pl.BlockSpec((1, d), fixed), pl.BlockSpec((1, d), fixed)],
        out_specs=pl.BlockSpec((tm, d), row),
        out_shape=jax.ShapeDtypeStruct((n, d), F32),
        compiler_params=_params("arbitrary"),
        name="proj_ln",
    )(o, w, x, g, b)


def _ffn_kernel(x_ref, wg_ref, wu_ref, wd_ref, g_ref, b_ref, y_ref, x_bf, *, alpha):
    j = pl.program_id(1)

    @pl.when(j == 0)
    def _():
        x_bf[...] = x_ref[...].astype(BF16)
        y_ref[...] = jnp.zeros_like(y_ref)

    for r in range(0, x_bf.shape[0], FFN_ROW_CHUNK):
        rows = slice(r, r + FFN_ROW_CHUNK)
        xb = x_bf[rows, :]
        gate = _dot(xb, wg_ref[...])
        up = _dot(xb, wu_ref[...])
        y_ref[rows, :] += _dot((gate * jax.nn.sigmoid(gate) * up).astype(BF16), wd_ref[...])

    @pl.when(j == pl.num_programs(1) - 1)
    def _():
        for r in range(0, x_bf.shape[0], LN_ROW_CHUNK):
            rows = slice(r, r + LN_ROW_CHUNK)
            y_ref[rows, :] = _layer_norm(alpha * x_ref[rows, :] + y_ref[rows, :], g_ref[...], b_ref[...])


def _ffn(x, w_gu, w_down, layer, g, b, alpha, tm=1024, tf=512):
    n, d = x.shape
    hidden = w_down.shape[1]
    nf = hidden // tf
    return pl.pallas_call(
        functools.partial(_ffn_kernel, alpha=alpha),
        grid=(n // tm, nf),
        in_specs=[pl.BlockSpec((tm, d), lambda i, j: (i, 0)),
                  pl.BlockSpec((None, d, tf), lambda i, j: (layer, 0, j)),
                  pl.BlockSpec((None, d, tf), lambda i, j: (layer, 0, nf + j)),
                  pl.BlockSpec((None, tf, d), lambda i, j: (layer, j, 0)),
                  pl.BlockSpec((1, d), lambda i, j: (0, 0)), pl.BlockSpec((1, d), lambda i, j: (0, 0))],
        out_specs=pl.BlockSpec((tm, d), lambda i, j: (i, 0)),
        out_shape=jax.ShapeDtypeStruct((n, d), F32),
        scratch_shapes=[pltpu.VMEM((tm, d), BF16)],
        compiler_params=_params("arbitrary", "arbitrary"),
        name="ffn",
    )(x, w_gu, w_gu, w_down, g, b)


def _mla_weights(w_in, w_q_up, w_kv_up):
    d = w_in.shape[0]
    half = MLA_ROPE_DIM // 2
    base = MLA_Q_RANK + MLA_KV_RANK
    k1, k2 = w_in[:, base:base + half], w_in[:, base + half:]
    w_in_cat = jnp.concatenate([w_in[:, :base], k1, k2, k1, k2, -k2, k1, -k2, k1], axis=1)
    wq = w_q_up.reshape(MLA_Q_RANK, MLA_HEADS, MLA_NOPE_DIM + MLA_ROPE_DIM)
    r1 = wq[:, :, MLA_NOPE_DIM:MLA_NOPE_DIM + half]
    r2 = wq[:, :, MLA_NOPE_DIM + half:]
    w_q_cat = jnp.concatenate([
        wq[:, :, :MLA_NOPE_DIM].reshape(MLA_Q_RANK, -1),
        wq[:, :, MLA_NOPE_DIM:].reshape(MLA_Q_RANK, -1),
        jnp.concatenate([-r2, r1], axis=2).reshape(MLA_Q_RANK, -1)], axis=1)
    wkv = w_kv_up.reshape(MLA_KV_RANK, MLA_HEADS, MLA_NOPE_DIM + MLA_V_DIM)
    w_kv_cat = jnp.concatenate([wkv[:, :, :MLA_NOPE_DIM].reshape(MLA_KV_RANK, -1),
                                wkv[:, :, MLA_NOPE_DIM:].reshape(MLA_KV_RANK, -1)], axis=1)
    del d
    return w_in_cat.astype(BF16), w_q_cat.astype(BF16), w_kv_cat.astype(BF16)


def _cast_blocks(rows, steps):
    return max(b for b in range(1, steps + 1) if steps % b == 0 and rows % (16 * b) == 0)


def _causal_bias(tq, granularity):
    r = np.arange(tq)[:, None] // granularity
    c = np.arange(tq)[None, :] // granularity
    return jnp.asarray(np.where(c <= r, 0.0, NEG_INF), F32)


def _mla_lane_mask():
    lane = np.arange(LANES)[None, :]
    head = np.arange(MLA_HEADS)[:, None]
    m = (lane // MLA_ROPE_DIM == head % 2).astype(np.float32)
    return jnp.asarray(np.broadcast_to(m[:, None, :], (MLA_HEADS, 8, LANES)), BF16)


def _fox_lane_mask():
    lane = np.arange(LANES)[None, :]
    head = np.arange(FOX_HEADS)[:, None]
    m = ((lane % FOX_HEADS == head) & (lane < 6 * FOX_HEADS)).astype(np.float32)
    return jnp.asarray(np.broadcast_to(m[:, None, :], (FOX_HEADS, 8, LANES)), BF16)


def _mla_layer(x, pos, w_in, q_norm_g, w_q_up, kv_norm_g, w_kv_up, w_o, batch, cast_jobs):
    w_in_cat, w_q_cat, w_kv_cat = _mla_weights(w_in, w_q_up, w_kv_up)
    half = MLA_ROPE_DIM // 2
    inv_freq = ROPE_THETA ** (-jnp.arange(0, MLA_ROPE_DIM, 2, dtype=F32) / MLA_ROPE_DIM)
    freq = jnp.tile(inv_freq, LANES // half)[None, :]
    cq, ckv, kr, cos, sin = _mla_in(x, pos, freq, w_in_cat, q_norm_g[None, :], kv_norm_g[None, :])
    scale = (MLA_NOPE_DIM + MLA_ROPE_DIM) ** -0.5 * LOG2E
    qn, qr = _mla_q(cq, cos, sin, w_q_cat, scale)
    kv = _matmul(ckv, w_kv_cat, w_kv_cat.shape[1], tm=1024, tn=1024)
    o, copies = _attention(qn, 0, qr, ATTN_HEADS_PER_STEP // 2, _mla_lane_mask(), kv, 0, kr, kv, MLA_HEADS,
                           _causal_bias(512, CHUNK), batch, MLA_HEADS, cast_jobs)
    return o, w_o.astype(BF16), copies


def _fox_layer(x, w_in, b_f, w_o, batch, cast_jobs):
    width = FOX_HEADS * FOX_HEAD_DIM
    w_bf = w_in.astype(BF16)
    qkv = _matmul(x, w_bf, 3 * width, tm=1024, tn=1024,
                  n_scaled=width // 1024, scale=FOX_HEAD_DIM ** -0.5 * LOG2E)
    groups = 6
    w_f = jnp.pad(jnp.tile(w_bf[:, 3 * width:], (1, groups)), ((0, 0), (0, LANES - groups * FOX_HEADS)))
    b6 = jnp.pad(jnp.tile(b_f, groups), (0, LANES - groups * FOX_HEADS))[None, :]
    aq, ak = _fox_gate(x, w_f, b6, batch)
    o, copies = _attention(qkv, 0, aq, 0, _fox_lane_mask(), qkv, FOX_HEADS, ak, qkv, 2 * FOX_HEADS,
                           _causal_bias(512, 1), batch, FOX_HEADS, cast_jobs)
    return o, w_o.astype(BF16), copies


def kernel(x, positions, mla_w_in, mla_q_norm_g, mla_w_q_up, mla_kv_norm_g, mla_w_kv_up, mla_w_o,
           fox_w_in, fox_b_f, fox_w_o, ffn_w_gu, ffn_w_down, ln_mix_g, ln_mix_b, ln_ffn_g, ln_ffn_b):
    batch, seq, d = x.shape
    depth = ffn_w_gu.shape[0]
    alpha = float((2 * depth) ** 0.25)
    h = x.reshape(batch * seq, d)
    pos = positions.astype(F32).reshape(batch * seq, 1)
    cast_jobs = [ffn_w_gu.reshape(-1, ffn_w_gu.shape[2]), ffn_w_down.reshape(-1, ffn_w_down.shape[2])]
    for i in range(depth):
        j = i // 2
        if i % 2 == 0:
            o, w_o, copies = _mla_layer(h, pos, mla_w_in[j], mla_q_norm_g[j], mla_w_q_up[j], mla_kv_norm_g[j],
                                        mla_w_kv_up[j], mla_w_o[j], batch, cast_jobs)
        else:
            o, w_o, copies = _fox_layer(h, fox_w_in[j], fox_b_f[j], fox_w_o[j], batch, cast_jobs)
        if cast_jobs:
            w_gu_bf, w_down_bf = copies[0].reshape(ffn_w_gu.shape), copies[1].reshape(ffn_w_down.shape)
            cast_jobs = []
        h = _proj_ln(o, w_o, h, ln_mix_g[i][None, :], ln_mix_b[i][None, :], alpha)
        h = _ffn(h, w_gu_bf, w_down_bf, i, ln_ffn_g[i][None, :], ln_ffn_b[i][None, :], alpha)
    return h.reshape(batch, seq, d)
```

```python
import functools

import jax
import jax.numpy as jnp
import numpy as np
from jax import lax
from jax.experimental import pallas as pl
from jax.experimental.pallas import tpu as pltpu

CHUNK = 64
MLA_HEADS = 16
MLA_NOPE_DIM = 128
MLA_ROPE_DIM = 64
MLA_V_DIM = 128
MLA_Q_RANK = 512
MLA_KV_RANK = 512
ROPE_THETA = 10000.0
FOX_HEADS = 16
FOX_HEAD_DIM = 128
LN_EPS = 1e-5
RMS_EPS = 1e-6
NEG_INF = -1e30
LOG2E = 1.4426950408889634

LANES = 128
VMEM_LIMIT_BYTES = 62 * 1024 * 1024
ATTN_HEADS_PER_STEP = 4
ATTN_TILES_PER_BLOCK = 4
FFN_ROW_CHUNK = 512
LN_ROW_CHUNK = 256

BF16 = jnp.bfloat16
F32 = jnp.float32


def _params(*semantics):
    return pltpu.CompilerParams(dimension_semantics=semantics, vmem_limit_bytes=VMEM_LIMIT_BYTES)


def _dot(a, b):
    return jnp.dot(a, b, preferred_element_type=F32)


def _layer_norm(y, g, b):
    mu = jnp.mean(y, axis=-1, keepdims=True)
    d = y - mu
    var = jnp.mean(d * d, axis=-1, keepdims=True)
    return d * lax.rsqrt(var + LN_EPS) * g + b


def _rms_norm(y, g):
    return y * lax.rsqrt(jnp.mean(y * y, axis=-1, keepdims=True) + RMS_EPS) * g


def _mla_in_kernel(x_ref, pos_ref, freq_ref, w_ref, gq_ref, gkv_ref,
                   cq_ref, ckv_ref, kr_ref, cos_ref, sin_ref):
    h = _dot(x_ref[...].astype(BF16), w_ref[...])
    cq_ref[...] = _rms_norm(h[:, :MLA_Q_RANK], gq_ref[...]).astype(BF16)
    ckv_ref[...] = _rms_norm(h[:, MLA_Q_RANK:MLA_Q_RANK + MLA_KV_RANK], gkv_ref[...]).astype(BF16)
    ang = pos_ref[...] * freq_ref[...]
    c = jnp.cos(ang)
    s = jnp.sin(ang)
    base = MLA_Q_RANK + MLA_KV_RANK
    kr_ref[...] = (h[:, base:base + LANES] * c + h[:, base + LANES:] * s).astype(BF16)
    cos_ref[...] = c
    sin_ref[...] = s


def _mla_in(x, pos, freq, w, gq, gkv, tm=512):
    n, d = x.shape
    row = lambda i: (i, 0)
    fixed = lambda i: (0, 0)
    return pl.pallas_call(
        _mla_in_kernel,
        grid=(n // tm,),
        in_specs=[pl.BlockSpec((tm, d), row), pl.BlockSpec((tm, 1), row),
                  pl.BlockSpec((1, LANES), fixed), pl.BlockSpec(w.shape, fixed),
                  pl.BlockSpec((1, MLA_Q_RANK), fixed), pl.BlockSpec((1, MLA_KV_RANK), fixed)],
        out_specs=[pl.BlockSpec((tm, MLA_Q_RANK), row), pl.BlockSpec((tm, MLA_KV_RANK), row),
                   pl.BlockSpec((tm, LANES), row), pl.BlockSpec((tm, LANES), row),
                   pl.BlockSpec((tm, LANES), row)],
        out_shape=[jax.ShapeDtypeStruct((n, MLA_Q_RANK), BF16),
                   jax.ShapeDtypeStruct((n, MLA_KV_RANK), BF16),
                   jax.ShapeDtypeStruct((n, LANES), BF16),
                   jax.ShapeDtypeStruct((n, LANES), F32),
                   jax.ShapeDtypeStruct((n, LANES), F32)],
        compiler_params=_params("arbitrary"),
        name="mla_in",
    )(x, pos, freq, w, gq, gkv)


def _mla_q_kernel(cq_ref, cos_ref, sin_ref, w_ref, qn_ref, qr_ref, *, scale):
    n_nope = MLA_HEADS * MLA_NOPE_DIM
    n_rope = MLA_HEADS * MLA_ROPE_DIM
    cq = cq_ref[...]
    qn_ref[...] = (_dot(cq, w_ref[:, :n_nope]) * scale).astype(BF16)
    r = _dot(cq, w_ref[:, n_nope:n_nope + n_rope])
    rp = _dot(cq, w_ref[:, n_nope + n_rope:])
    c = cos_ref[...]
    s = sin_ref[...]
    for p in range(n_rope // LANES):
        sl = slice(p * LANES, (p + 1) * LANES)
        qr_ref[:, sl] = ((r[:, sl] * c + rp[:, sl] * s) * scale).astype(BF16)


def _mla_q(cq, cos, sin, w, scale, tm=512):
    n = cq.shape[0]
    n_nope = MLA_HEADS * MLA_NOPE_DIM
    n_rope = MLA_HEADS * MLA_ROPE_DIM
    row = lambda i: (i, 0)
    fixed = lambda i: (0, 0)
    return pl.pallas_call(
        functools.partial(_mla_q_kernel, scale=scale),
        grid=(n // tm,),
        in_specs=[pl.BlockSpec((tm, MLA_Q_RANK), row), pl.BlockSpec((tm, LANES), row),
                  pl.BlockSpec((tm, LANES), row), pl.BlockSpec(w.shape, fixed)],
        out_specs=[pl.BlockSpec((tm, n_nope), row), pl.BlockSpec((tm, n_rope), row)],
        out_shape=[jax.ShapeDtypeStruct((n, n_nope), BF16), jax.ShapeDtypeStruct((n, n_rope), BF16)],
        compiler_params=_params("arbitrary"),
        name="mla_q_up",
    )(cq, cos, sin, w)


def _matmul_kernel(a_ref, b_ref, o_ref, a_bf, *, n_scaled, scale):
    j = pl.program_id(1)

    @pl.when(j == 0)
    def _():
        a_bf[...] = a_ref[...].astype(BF16)

    acc = _dot(a_bf[...], b_ref[...])
    if n_scaled:
        acc = acc * jnp.where(j < n_scaled, scale, 1.0).astype(F32)
    o_ref[...] = acc.astype(o_ref.dtype)


def _matmul(a, b, n, tm, tn, n_scaled=0, scale=1.0):
    m, k = a.shape
    return pl.pallas_call(
        functools.partial(_matmul_kernel, n_scaled=n_scaled, scale=scale),
        grid=(m // tm, n // tn),
        in_specs=[pl.BlockSpec((tm, k), lambda i, j: (i, 0)), pl.BlockSpec((k, tn), lambda i, j: (0, j))],
        out_specs=pl.BlockSpec((tm, tn), lambda i, j: (i, j)),
        out_shape=jax.ShapeDtypeStruct((m, n), BF16),
        scratch_shapes=[pltpu.VMEM((tm, k), BF16)],
        compiler_params=_params("arbitrary", "arbitrary"),
        name="matmul",
    )(a, b)


def _split3(v):
    hi = v.astype(BF16).astype(F32)
    r1 = v - hi
    mid = r1.astype(BF16).astype(F32)
    lo = (r1 - mid).astype(BF16).astype(F32)
    return hi, mid, lo


def _fox_gate_kernel(x_ref, w_ref, b_ref, tri_ref, aq_ref, ak_ref, carry):
    t = pl.program_id(1)

    @pl.when(t == 0)
    def _():
        carry[...] = jnp.zeros_like(carry)

    logit = _dot(x_ref[...].astype(BF16), w_ref[...]) + b_ref[...]
    log_f = jnp.minimum(logit, 0.0) - jnp.log1p(jnp.exp(-jnp.abs(logit)))
    tri = tri_ref[...]
    hi, mid, lo = _split3(log_f)
    c = (_dot(tri, hi.astype(BF16)) + _dot(tri, mid.astype(BF16))) + _dot(tri, lo.astype(BF16))
    c = c + carry[0:1, :]
    ts = c.shape[0]
    carry[0:1, :] = c[ts - 1:ts, :]
    hi, mid, lo = _split3(c * LOG2E)
    group = lax.broadcasted_iota(jnp.int32, c.shape, 1) // FOX_HEADS
    pieces = jnp.where(group % 3 == 0, hi, jnp.where(group % 3 == 1, mid, lo))
    aq_ref[...] = jnp.where(group < 3, pieces, jnp.where(group < 6, 1.0, 0.0)).astype(BF16)
    ak_ref[...] = jnp.where(group < 3, 1.0, jnp.where(group < 6, -pieces, 0.0)).astype(BF16)


def _fox_gate(x, w, b, batch, ts=512):
    n, d = x.shape
    steps = n // batch // ts
    tri = jnp.asarray(np.tril(np.ones((ts, ts), np.float32)), BF16)
    row = lambda bi, t: (bi * steps + t, 0)
    fixed = lambda bi, t: (0, 0)
    return pl.pallas_call(
        _fox_gate_kernel,
        grid=(batch, steps),
        in_specs=[pl.BlockSpec((ts, d), row), pl.BlockSpec(w.shape, fixed),
                  pl.BlockSpec((1, LANES), fixed), pl.BlockSpec((ts, ts), fixed)],
        out_specs=[pl.BlockSpec((ts, LANES), row), pl.BlockSpec((ts, LANES), row)],
        out_shape=[jax.ShapeDtypeStruct((n, LANES), BF16), jax.ShapeDtypeStruct((n, LANES), BF16)],
        scratch_shapes=[pltpu.VMEM((8, LANES), F32)],
        compiler_params=_params("arbitrary", "arbitrary"),
        name="fox_gate",
    )(x, w, b, tri)


def _attn_kernel(qm_ref, qa_ref, lm_ref, km_ref, ka_ref, v_ref, bias_ref, *rest, tq, hpb, n_cast):
    cast_in, (o_ref, *cast_out), (kcat, vcat, m_sc, acc_sc) = rest[:n_cast], rest[n_cast:2 * n_cast + 1], rest[2 * n_cast + 1:]
    i = pl.program_id(2)
    for src, dst in zip(cast_in, cast_out):
        dst[...] = src[...].astype(BF16)

    @pl.when(i == 0)
    def _():
        ones = jnp.ones(ka_ref.shape, BF16)
        for g in range(hpb):
            head = slice(g * LANES, (g + 1) * LANES)
            kcat[g, :, :LANES] = km_ref[:, head]
            kcat[g, :, LANES:] = ka_ref[...]
            vcat[g, :, :LANES] = v_ref[:, head]
            vcat[g, :, LANES:] = ones

    qa_tiles = qa_ref.shape[1] // LANES

    def q_tile(g):
        t = g * qa_tiles // hpb
        aux = qa_ref[:, t * LANES:(t + 1) * LANES] * lm_ref[g, 0:1, :]
        return jnp.concatenate([qm_ref[:, g * LANES:(g + 1) * LANES], aux], axis=1)

    q = [q_tile(g) for g in range(hpb)]

    nt = (((1,), (1,)), ((), ()))
    half = tq // 2

    def row_max(s):
        return jnp.broadcast_to(jnp.max(s, axis=-1, keepdims=True), (s.shape[0], LANES))

    def tile(g, j):
        rows = pl.ds(pl.multiple_of(j * tq, tq), tq)
        s = lax.dot_general(q[g], kcat[g, rows, :], nt, preferred_element_type=F32)
        m_new = jnp.maximum(m_sc[g], row_max(s))
        p = jnp.exp2(s - jnp.tile(m_new, (1, tq // LANES))).astype(BF16)
        pv = _dot(p, vcat[g, rows, :])
        acc_sc[g] = jnp.tile(jnp.exp2(m_sc[g] - m_new), (1, 2)) * acc_sc[g] + pv
        m_sc[g] = m_new

    def diagonal_tile(g, tri):
        left = pl.ds(pl.multiple_of(i * tq, tq), half)
        right = pl.ds(pl.multiple_of(i * tq + half, half), half)
        s_l = lax.dot_general(q[g], kcat[g, left, :], nt, preferred_element_type=F32)
        s_l = jnp.concatenate([s_l[:half] + tri, s_l[half:]], axis=0)
        s_r = lax.dot_general(q[g][half:], kcat[g, right, :], nt, preferred_element_type=F32) + tri
        m_top = row_max(s_l[:half])
        m_bot = jnp.maximum(row_max(s_l[half:]), row_max(s_r))
        m_new = jnp.concatenate([m_top, m_bot], axis=0)
        p_l = jnp.exp2(s_l - jnp.tile(m_new, (1, half // LANES))).astype(BF16)
        p_r = jnp.exp2(s_r - jnp.tile(m_bot, (1, half // LANES))).astype(BF16)
        pv_l = _dot(p_l, vcat[g, left, :])
        acc_sc[g, :half] = pv_l[:half]
        acc_sc[g, half:] = pv_l[half:] + _dot(p_r, vcat[g, right, :])
        m_sc[g] = m_new

    def tiles(js, with_diagonal=False):
        if with_diagonal:
            tri = bias_ref[...]
            for g in range(hpb):
                diagonal_tile(g, tri)
        for j in js:
            for g in range(hpb):
                tile(g, j)

    group = ATTN_TILES_PER_BLOCK
    rem = i % group
    for r in range(group):
        @pl.when(rem == r)
        def _(r=r):
            tiles([i - t for t in range(1, r + 1)], with_diagonal=True)

    def body(jj, carry):
        tiles([group * jj + t for t in range(group)])
        return carry

    lax.fori_loop(0, i // group, body, 0)
    for g in range(hpb):
        acc = acc_sc[g]
        o_ref[:, g * LANES:(g + 1) * LANES] = (acc[:, :LANES] / acc[:, LANES:]).astype(o_ref.dtype)


def _attention(qm, qm_col0, qa, qa_tiles, lane_mask, km, km_col0, ka, v, v_col0, granularity, batch, heads,
               cast_jobs=(), tq=512, hpb=ATTN_HEADS_PER_STEP):
    n = qm.shape[0]
    seq = n // batch
    nq = seq // tq
    wide = hpb * LANES
    head_blocks = heads // hpb
    steps = batch * head_blocks * nq
    q_row = lambda b, i: b * nq + i
    bias = _causal_bias(tq // 2, granularity)

    def cast_spec(arr):
        blocks = _cast_blocks(arr.shape[0], steps)
        rows, repeat = arr.shape[0] // blocks, steps // blocks
        return pl.BlockSpec((rows, arr.shape[1]), lambda b, h, i: (((b * head_blocks + h) * nq + i) // repeat, 0))

    cast_specs = [cast_spec(arr) for arr in cast_jobs]
    out, *copies = pl.pallas_call(
        functools.partial(_attn_kernel, tq=tq, hpb=hpb, n_cast=len(cast_jobs)),
        grid=(batch, head_blocks, nq),
        in_specs=[
            pl.BlockSpec((tq, wide), lambda b, h, i: (q_row(b, i), qm_col0 // hpb + h)),
            pl.BlockSpec((tq, max(qa_tiles, 1) * LANES), lambda b, h, i: (q_row(b, i), h if qa_tiles else 0)),
            pl.BlockSpec((hpb, 8, LANES), lambda b, h, i: (h, 0, 0)),
            pl.BlockSpec((seq, wide), lambda b, h, i: (b, km_col0 // hpb + h)),
            pl.BlockSpec((seq, LANES), lambda b, h, i: (b, 0)),
            pl.BlockSpec((seq, wide), lambda b, h, i: (b, v_col0 // hpb + h)),
            pl.BlockSpec(bias.shape, lambda b, h, i: (0, 0)),
        ] + cast_specs,
        out_specs=[pl.BlockSpec((tq, wide), lambda b, h, i: (q_row(b, i), h))] + cast_specs,
        out_shape=[jax.ShapeDtypeStruct((n, heads * LANES), BF16)]
        + [jax.ShapeDtypeStruct(arr.shape, BF16) for arr in cast_jobs],
        scratch_shapes=[pltpu.VMEM((hpb, seq, 2 * LANES), BF16), pltpu.VMEM((hpb, seq, 2 * LANES), BF16),
                        pltpu.VMEM((hpb, tq, LANES), F32), pltpu.VMEM((hpb, tq, 2 * LANES), F32)],
        compiler_params=_params("arbitrary", "arbitrary", "arbitrary"),
        name="attention",
    )(qm, qa, lane_mask, km, ka, v, bias, *cast_jobs)
    return out, copies


def _proj_ln_kernel(o_ref, w_ref, x_ref, g_ref, b_ref, y_ref, *, alpha):
    y = alpha * x_ref[...] + _dot(o_ref[...], w_ref[...])
    y_ref[...] = _layer_norm(y, g_ref[...], b_ref[...])


def _proj_ln(o, w, x, g, b, alpha, tm=512):
    n, d = x.shape
    row = lambda i: (i, 0)
    fixed = lambda i: (0, 0)
    return pl.pallas_call(
        functools.partial(_proj_ln_kernel, alpha=alpha),
        grid=(n // tm,),
        in_specs=[pl.BlockSpec((tm, o.shape[1]), row), pl.BlockSpec(w.shape, fixed),
                  pl.BlockSpec((tm, d), row), pl.BlockSpec((1, d), fixed), pl.BlockSpec((1, d), fixed)],
        out_specs=pl.BlockSpec((tm, d), row),
        out_shape=jax.ShapeDtypeStruct((n, d), F32),
        compiler_params=_params("arbitrary"),
        name="proj_ln",
    )(o, w, x, g, b)


def _ffn_kernel(x_ref, wg_ref, wu_ref, wd_ref, g_ref, b_ref, y_ref, x_bf, *, alpha):
    j = pl.program_id(1)

    @pl.when(j == 0)
    def _():
        x_bf[...] = x_ref[...].astype(BF16)
        y_ref[...] = jnp.zeros_like(y_ref)

    for r in range(0, x_bf.shape[0], FFN_ROW_CHUNK):
        rows = slice(r, r + FFN_ROW_CHUNK)
        xb = x_bf[rows, :]
        gate = _dot(xb, wg_ref[...])
        up = _dot(xb, wu_ref[...])
        y_ref[rows, :] += _dot((gate * jax.nn.sigmoid(gate) * up).astype(BF16), wd_ref[...])

    @pl.when(j == pl.num_programs(1) - 1)
    def _():
        for r in range(0, x_bf.shape[0], LN_ROW_CHUNK):
            rows = slice(r, r + LN_ROW_CHUNK)
            y_ref[rows, :] = _layer_norm(alpha * x_ref[rows, :] + y_ref[rows, :], g_ref[...], b_ref[...])


def _ffn(x, w_gu, w_down, layer, g, b, alpha, tm=1024, tf=512):
    n, d = x.shape
    hidden = w_down.shape[1]
    nf = hidden // tf
    return pl.pallas_call(
        functools.partial(_ffn_kernel, alpha=alpha),
        grid=(n // tm, nf),
        in_specs=[pl.BlockSpec((tm, d), lambda i, j: (i, 0)),
                  pl.BlockSpec((None, d, tf), lambda i, j: (layer, 0, j)),
                  pl.BlockSpec((None, d, tf), lambda i, j: (layer, 0, nf + j)),
                  pl.BlockSpec((None, tf, d), lambda i, j: (layer, j, 0)),
                  pl.BlockSpec((1, d), lambda i, j: (0, 0)), pl.BlockSpec((1, d), lambda i, j: (0, 0))],
        out_specs=pl.BlockSpec((tm, d), lambda i, j: (i, 0)),
        out_shape=jax.ShapeDtypeStruct((n, d), F32),
        scratch_shapes=[pltpu.VMEM((tm, d), BF16)],
        compiler_params=_params("arbitrary", "arbitrary"),
        name="ffn",
    )(x, w_gu, w_gu, w_down, g, b)


def _mla_weights(w_in, w_q_up, w_kv_up):
    d = w_in.shape[0]
    half = MLA_ROPE_DIM // 2
    base = MLA_Q_RANK + MLA_KV_RANK
    k1, k2 = w_in[:, base:base + half], w_in[:, base + half:]
    w_in_cat = jnp.concatenate([w_in[:, :base], k1, k2, k1, k2, -k2, k1, -k2, k1], axis=1)
    wq = w_q_up.reshape(MLA_Q_RANK, MLA_HEADS, MLA_NOPE_DIM + MLA_ROPE_DIM)
    r1 = wq[:, :, MLA_NOPE_DIM:MLA_NOPE_DIM + half]
    r2 = wq[:, :, MLA_NOPE_DIM + half:]
    w_q_cat = jnp.concatenate([
        wq[:, :, :MLA_NOPE_DIM].reshape(MLA_Q_RANK, -1),
        wq[:, :, MLA_NOPE_DIM:].reshape(MLA_Q_RANK, -1),
        jnp.concatenate([-r2, r1], axis=2).reshape(MLA_Q_RANK, -1)], axis=1)
    wkv = w_kv_up.reshape(MLA_KV_RANK, MLA_HEADS, MLA_NOPE_DIM + MLA_V_DIM)
    w_kv_cat = jnp.concatenate([wkv[:, :, :MLA_NOPE_DIM].reshape(MLA_KV_RANK, -1),
                                wkv[:, :, MLA_NOPE_DIM:].reshape(MLA_KV_RANK, -1)], axis=1)
    del d
    return w_in_cat.astype(BF16), w_q_cat.astype(BF16), w_kv_cat.astype(BF16)


def _cast_blocks(rows, steps):
    return max(b for b in range(1, steps + 1) if steps % b == 0 and rows % (16 * b) == 0)


def _causal_bias(tq, granularity):
    r = np.arange(tq)[:, None] // granularity
    c = np.arange(tq)[None, :] // granularity
    return jnp.asarray(np.where(c <= r, 0.0, NEG_INF), F32)


def _mla_lane_mask():
    lane = np.arange(LANES)[None, :]
    head = np.arange(MLA_HEADS)[:, None]
    m = (lane // MLA_ROPE_DIM == head % 2).astype(np.float32)
    return jnp.asarray(np.broadcast_to(m[:, None, :], (MLA_HEADS, 8, LANES)), BF16)


def _fox_lane_mask():
    lane = np.arange(LANES)[None, :]
    head = np.arange(FOX_HEADS)[:, None]
    m = ((lane % FOX_HEADS == head) & (lane < 6 * FOX_HEADS)).astype(np.float32)
    return jnp.asarray(np.broadcast_to(m[:, None, :], (FOX_HEADS, 8, LANES)), BF16)


def _mla_layer(x, pos, w_in, q_norm_g, w_q_up, kv_norm_g, w_kv_up, w_o, batch, cast_jobs):
    w_in_cat, w_q_cat, w_kv_cat = _mla_weights(w_in, w_q_up, w_kv_up)
    half = MLA_ROPE_DIM // 2
    inv_freq = ROPE_THETA ** (-jnp.arange(0, MLA_ROPE_DIM, 2, dtype=F32) / MLA_ROPE_DIM)
    freq = jnp.tile(inv_freq, LANES // half)[None, :]
    cq, ckv, kr, cos, sin = _mla_in(x, pos, freq, w_in_cat, q_norm_g[None, :], kv_norm_g[None, :])
    scale = (MLA_NOPE_DIM + MLA_ROPE_DIM) ** -0.5 * LOG2E
    qn, qr = _mla_q(cq, cos, sin, w_q_cat, scale)
    kv = _matmul(ckv, w_kv_cat, w_kv_cat.shape[1], tm=1024, tn=2048)
    o, copies = _attention(qn, 0, qr, ATTN_HEADS_PER_STEP // 2, _mla_lane_mask(), kv, 0, kr, kv, MLA_HEADS,
                           CHUNK, batch, MLA_HEADS, cast_jobs)
    return o, w_o.astype(BF16), copies


def _fox_layer(x, w_in, b_f, w_o, batch, cast_jobs):
    width = FOX_HEADS * FOX_HEAD_DIM
    w_bf = w_in.astype(BF16)
    qkv = _matmul(x, w_bf, 3 * width, tm=1024, tn=width,
                  n_scaled=1, scale=FOX_HEAD_DIM ** -0.5 * LOG2E)
    groups = 6
    w_f = jnp.pad(jnp.tile(w_bf[:, 3 * width:], (1, groups)), ((0, 0), (0, LANES - groups * FOX_HEADS)))
    b6 = jnp.pad(jnp.tile(b_f, groups), (0, LANES - groups * FOX_HEADS))[None, :]
    aq, ak = _fox_gate(x, w_f, b6, batch)
    o, copies = _attention(qkv, 0, aq, 0, _fox_lane_mask(), qkv, FOX_HEADS, ak, qkv, 2 * FOX_HEADS,
                           1, batch, FOX_HEADS, cast_jobs)
    return o, w_o.astype(BF16), copies


def kernel(x, positions, mla_w_in, mla_q_norm_g, mla_w_q_up, mla_kv_norm_g, mla_w_kv_up, mla_w_o,
           fox_w_in, fox_b_f, fox_w_o, ffn_w_gu, ffn_w_down, ln_mix_g, ln_mix_b, ln_ffn_g, ln_ffn_b):
    batch, seq, d = x.shape
    depth = ffn_w_gu.shape[0]
    alpha = float((2 * depth) ** 0.25)
    h = x.reshape(batch * seq, d)
    pos = positions.astype(F32).reshape(batch * seq, 1)
    cast_jobs = [ffn_w_gu.reshape(-1, ffn_w_gu.shape[2]), ffn_w_down.reshape(-1, ffn_w_down.shape[2])]
    for i in range(depth):
        j = i // 2
        if i % 2 == 0:
            o, w_o, copies = _mla_layer(h, pos, mla_w_in[j], mla_q_norm_g[j], mla_w_q_up[j], mla_kv_norm_g[j],
                                        mla_w_kv_up[j], mla_w_o[j], batch, cast_jobs)
        else:
            o, w_o, copies = _fox_layer(h, fox_w_in[j], fox_b_f[j], fox_w_o[j], batch, cast_jobs)
        if cast_jobs:
            w_gu_bf, w_down_bf = copies[0].reshape(ffn_w_gu.shape), copies[1].reshape(ffn_w_down.shape)
            cast_jobs = []
        h = _proj_ln(o, w_o, h, ln_mix_g[i][None, :], ln_mix_b[i][None, :], alpha)
        h = _ffn(h, w_gu_bf, w_down_bf, i, ln_ffn_g[i][None, :], ln_ffn_b[i][None, :], alpha)
    return h.reshape(batch, seq, d)
```

```python
import functools

import jax
import jax.numpy as jnp
import numpy as np
from jax import lax
from jax.experimental import pallas as pl
from jax.experimental.pallas import tpu as pltpu

CHUNK = 64
MLA_HEADS = 16
MLA_NOPE_DIM = 128
MLA_ROPE_DIM = 64
MLA_V_DIM = 128
MLA_Q_RANK = 512
MLA_KV_RANK = 512
ROPE_THETA = 10000.0
FOX_HEADS = 16
FOX_HEAD_DIM = 128
LN_EPS = 1e-5
RMS_EPS = 1e-6
NEG_INF = -1e30
LOG2E = 1.4426950408889634

LANES = 128
VMEM_LIMIT_BYTES = 62 * 1024 * 1024
ATTN_HEADS_PER_STEP = 4
ATTN_TILES_PER_BLOCK = 4
FFN_ROW_CHUNK = 512
LN_ROW_CHUNK = 256
MLA_KV_COL_CHUNK = 1024

BF16 = jnp.bfloat16
F32 = jnp.float32


def _params(*semantics):
    return pltpu.CompilerParams(dimension_semantics=semantics, vmem_limit_bytes=VMEM_LIMIT_BYTES)


def _dot(a, b):
    return jnp.dot(a, b, preferred_element_type=F32)


def _layer_norm(y, g, b):
    mu = jnp.mean(y, axis=-1, keepdims=True)
    d = y - mu
    var = jnp.mean(d * d, axis=-1, keepdims=True)
    return d * lax.rsqrt(var + LN_EPS) * g + b


def _rms_norm(y, g):
    return y * lax.rsqrt(jnp.mean(y * y, axis=-1, keepdims=True) + RMS_EPS) * g


def _mla_proj_kernel(x_ref, pos_ref, freq_ref, w_in_ref, gq_ref, gkv_ref, wq_ref, wkv_ref,
                     qn_ref, qr_ref, kv_ref, kr_ref, *, scale):
    n_nope = MLA_HEADS * MLA_NOPE_DIM
    n_rope = MLA_HEADS * MLA_ROPE_DIM
    base = MLA_Q_RANK + MLA_KV_RANK
    h = _dot(x_ref[...].astype(BF16), w_in_ref[...])
    cq = _rms_norm(h[:, :MLA_Q_RANK], gq_ref[...]).astype(BF16)
    ckv = _rms_norm(h[:, MLA_Q_RANK:base], gkv_ref[...]).astype(BF16)
    ang = pos_ref[...] * freq_ref[...]
    c = jnp.cos(ang)
    s = jnp.sin(ang)
    kr_ref[...] = (h[:, base:base + LANES] * c + h[:, base + LANES:] * s).astype(BF16)
    qn_ref[...] = (_dot(cq, wq_ref[:, :n_nope]) * scale).astype(BF16)
    r = _dot(cq, wq_ref[:, n_nope:n_nope + n_rope])
    rp = _dot(cq, wq_ref[:, n_nope + n_rope:])
    for p in range(n_rope // LANES):
        sl = slice(p * LANES, (p + 1) * LANES)
        qr_ref[:, sl] = ((r[:, sl] * c + rp[:, sl] * s) * scale).astype(BF16)
    for col in range(0, kv_ref.shape[1], MLA_KV_COL_CHUNK):
        cols = slice(col, col + MLA_KV_COL_CHUNK)
        kv_ref[:, cols] = _dot(ckv, wkv_ref[:, cols]).astype(BF16)


def _mla_proj(x, pos, freq, w_in, gq, gkv, wq, wkv, scale, tm=512):
    n, d = x.shape
    n_nope = MLA_HEADS * MLA_NOPE_DIM
    n_rope = MLA_HEADS * MLA_ROPE_DIM
    row = lambda i: (i, 0)
    fixed = lambda i: (0, 0)
    resident = lambda w: pl.BlockSpec(w.shape, fixed, pipeline_mode=pl.Buffered(1))
    return pl.pallas_call(
        functools.partial(_mla_proj_kernel, scale=scale),
        grid=(n // tm,),
        in_specs=[pl.BlockSpec((tm, d), row), pl.BlockSpec((tm, 1), row), pl.BlockSpec((1, LANES), fixed),
                  resident(w_in), pl.BlockSpec((1, MLA_Q_RANK), fixed), pl.BlockSpec((1, MLA_KV_RANK), fixed),
                  resident(wq), resident(wkv)],
        out_specs=[pl.BlockSpec((tm, n_nope), row), pl.BlockSpec((tm, n_rope), row),
                   pl.BlockSpec((tm, wkv.shape[1]), row), pl.BlockSpec((tm, LANES), row)],
        out_shape=[jax.ShapeDtypeStruct((n, n_nope), BF16), jax.ShapeDtypeStruct((n, n_rope), BF16),
                   jax.ShapeDtypeStruct((n, wkv.shape[1]), BF16), jax.ShapeDtypeStruct((n, LANES), BF16)],
        compiler_params=_params("arbitrary"),
        name="mla_proj",
    )(x, pos, freq, w_in, gq, gkv, wq, wkv)


def _matmul_kernel(a_ref, b_ref, o_ref, a_bf, *, n_scaled, scale):
    j = pl.program_id(1)

    @pl.when(j == 0)
    def _():
        a_bf[...] = a_ref[...].astype(BF16)

    acc = _dot(a_bf[...], b_ref[...])
    if n_scaled:
        acc = acc * jnp.where(j < n_scaled, scale, 1.0).astype(F32)
    o_ref[...] = acc.astype(o_ref.dtype)


def _matmul(a, b, n, tm, tn, n_scaled=0, scale=1.0):
    m, k = a.shape
    return pl.pallas_call(
        functools.partial(_matmul_kernel, n_scaled=n_scaled, scale=scale),
        grid=(m // tm, n // tn),
        in_specs=[pl.BlockSpec((tm, k), lambda i, j: (i, 0)), pl.BlockSpec((k, tn), lambda i, j: (0, j))],
        out_specs=pl.BlockSpec((tm, tn), lambda i, j: (i, j)),
        out_shape=jax.ShapeDtypeStruct((m, n), BF16),
        scratch_shapes=[pltpu.VMEM((tm, k), BF16)],
        compiler_params=_params("arbitrary", "arbitrary"),
        name="matmul",
    )(a, b)


def _split3(v):
    hi = v.astype(BF16).astype(F32)
    r1 = v - hi
    mid = r1.astype(BF16).astype(F32)
    lo = (r1 - mid).astype(BF16).astype(F32)
    return hi, mid, lo


def _fox_gate_kernel(x_ref, w_ref, b_ref, tri_ref, aq_ref, ak_ref, carry):
    t = pl.program_id(1)

    @pl.when(t == 0)
    def _():
        carry[...] = jnp.zeros_like(carry)

    logit = _dot(x_ref[...].astype(BF16), w_ref[...]) + b_ref[...]
    log_f = jnp.minimum(logit, 0.0) - jnp.log1p(jnp.exp(-jnp.abs(logit)))
    tri = tri_ref[...]
    hi, mid, lo = _split3(log_f)
    c = (_dot(tri, hi.astype(BF16)) + _dot(tri, mid.astype(BF16))) + _dot(tri, lo.astype(BF16))
    c = c + carry[0:1, :]
    ts = c.shape[0]
    carry[0:1, :] = c[ts - 1:ts, :]
    hi, mid, lo = _split3(c * LOG2E)
    group = lax.broadcasted_iota(jnp.int32, c.shape, 1) // FOX_HEADS
    pieces = jnp.where(group % 3 == 0, hi, jnp.where(group % 3 == 1, mid, lo))
    aq_ref[...] = jnp.where(group < 3, pieces, jnp.where(group < 6, 1.0, 0.0)).astype(BF16)
    ak_ref[...] = jnp.where(group < 3, 1.0, jnp.where(group < 6, -pieces, 0.0)).astype(BF16)


def _fox_gate(x, w, b, batch, ts=512):
    n, d = x.shape
    steps = n // batch // ts
    tri = jnp.asarray(np.tril(np.ones((ts, ts), np.float32)), BF16)
    row = lambda bi, t: (bi * steps + t, 0)
    fixed = lambda bi, t: (0, 0)
    return pl.pallas_call(
        _fox_gate_kernel,
        grid=(batch, steps),
        in_specs=[pl.BlockSpec((ts, d), row), pl.BlockSpec(w.shape, fixed),
                  pl.BlockSpec((1, LANES), fixed), pl.BlockSpec((ts, ts), fixed)],
        out_specs=[pl.BlockSpec((ts, LANES), row), pl.BlockSpec((ts, LANES), row)],
        out_shape=[jax.ShapeDtypeStruct((n, LANES), BF16), jax.ShapeDtypeStruct((n, LANES), BF16)],
        scratch_shapes=[pltpu.VMEM((8, LANES), F32)],
        compiler_params=_params("arbitrary", "arbitrary"),
        name="fox_gate",
    )(x, w, b, tri)


def _attn_kernel(qm_ref, qa_ref, lm_ref, km_ref, ka_ref, v_ref, bias_ref, *rest, tq, hpb, n_cast):
    cast_in, (o_ref, *cast_out), (kcat, vcat, m_sc, acc_sc) = rest[:n_cast], rest[n_cast:2 * n_cast + 1], rest[2 * n_cast + 1:]
    i = pl.program_id(2)
    for src, dst in zip(cast_in, cast_out):
        dst[...] = src[...].astype(BF16)

    @pl.when(i == 0)
    def _():
        ones = jnp.ones(ka_ref.shape, BF16)
        for g in range(hpb):
            head = slice(g * LANES, (g + 1) * LANES)
            kcat[g, :, :LANES] = km_ref[:, head]
            kcat[g, :, LANES:] = ka_ref[...]
            vcat[g, :, :LANES] = v_ref[:, head]
            vcat[g, :, LANES:] = ones

    qa_tiles = qa_ref.shape[1] // LANES

    def q_tile(g):
        t = g * qa_tiles // hpb
        aux = qa_ref[:, t * LANES:(t + 1) * LANES] * lm_ref[g, 0:1, :]
        return jnp.concatenate([qm_ref[:, g * LANES:(g + 1) * LANES], aux], axis=1)

    q = [q_tile(g) for g in range(hpb)]

    nt = (((1,), (1,)), ((), ()))
    half = tq // 2

    def row_max(s):
        return jnp.broadcast_to(jnp.max(s, axis=-1, keepdims=True), (s.shape[0], LANES))

    def tile(g, j):
        rows = pl.ds(pl.multiple_of(j * tq, tq), tq)
        s = lax.dot_general(q[g], kcat[g, rows, :], nt, preferred_element_type=F32)
        m_new = jnp.maximum(m_sc[g], row_max(s))
        p = jnp.exp2(s - jnp.tile(m_new, (1, tq // LANES))).astype(BF16)
        pv = _dot(p, vcat[g, rows, :])
        acc_sc[g] = jnp.tile(jnp.exp2(m_sc[g] - m_new), (1, 2)) * acc_sc[g] + pv
        m_sc[g] = m_new

    def diagonal_tile(g, tri):
        left = pl.ds(pl.multiple_of(i * tq, tq), half)
        right = pl.ds(pl.multiple_of(i * tq + half, half), half)
        s_l = lax.dot_general(q[g], kcat[g, left, :], nt, preferred_element_type=F32)
        s_l = jnp.concatenate([s_l[:half] + tri, s_l[half:]], axis=0)
        s_r = lax.dot_general(q[g][half:], kcat[g, right, :], nt, preferred_element_type=F32) + tri
        m_top = row_max(s_l[:half])
        m_bot = jnp.maximum(row_max(s_l[half:]), row_max(s_r))
        m_new = jnp.concatenate([m_top, m_bot], axis=0)
        p_l = jnp.exp2(s_l - jnp.tile(m_new, (1, half // LANES))).astype(BF16)
        p_r = jnp.exp2(s_r - jnp.tile(m_bot, (1, half // LANES))).astype(BF16)
        pv_l = _dot(p_l, vcat[g, left, :])
        acc_sc[g, :half] = pv_l[:half]
        acc_sc[g, half:] = pv_l[half:] + _dot(p_r, vcat[g, right, :])
        m_sc[g] = m_new

    def tiles(js, with_diagonal=False):
        if with_diagonal:
            tri = bias_ref[...]
            for g in range(hpb):
                diagonal_tile(g, tri)
        for j in js:
            for g in range(hpb):
                tile(g, j)

    group = ATTN_TILES_PER_BLOCK
    rem = i % group
    for r in range(group):
        @pl.when(rem == r)
        def _(r=r):
            tiles([i - t for t in range(1, r + 1)], with_diagonal=True)

    def body(jj, carry):
        tiles([group * jj + t for t in range(group)])
        return carry

    lax.fori_loop(0, i // group, body, 0)
    for g in range(hpb):
        acc = acc_sc[g]
        o_ref[:, g * LANES:(g + 1) * LANES] = (acc[:, :LANES] / acc[:, LANES:]).astype(o_ref.dtype)


def _attention(qm, qm_col0, qa, qa_tiles, lane_mask, km, km_col0, ka, v, v_col0, granularity, batch, heads,
               cast_jobs=(), tq=512, hpb=ATTN_HEADS_PER_STEP):
    n = qm.shape[0]
    seq = n // batch
    nq = seq // tq
    wide = hpb * LANES
    head_blocks = heads // hpb
    steps = batch * head_blocks * nq
    q_row = lambda b, i: b * nq + i
    bias = _causal_bias(tq // 2, granularity)

    def cast_spec(arr):
        blocks = _cast_blocks(arr.shape[0], steps)
        rows, repeat = arr.shape[0] // blocks, steps // blocks
        return pl.BlockSpec((rows, arr.shape[1]), lambda b, h, i: (((b * head_blocks + h) * nq + i) // repeat, 0))

    cast_specs = [cast_spec(arr) for arr in cast_jobs]
    out, *copies = pl.pallas_call(
        functools.partial(_attn_kernel, tq=tq, hpb=hpb, n_cast=len(cast_jobs)),
        grid=(batch, head_blocks, nq),
        in_specs=[
            pl.BlockSpec((tq, wide), lambda b, h, i: (q_row(b, i), qm_col0 // hpb + h)),
            pl.BlockSpec((tq, max(qa_tiles, 1) * LANES), lambda b, h, i: (q_row(b, i), h if qa_tiles else 0)),
            pl.BlockSpec((hpb, 8, LANES), lambda b, h, i: (h, 0, 0)),
            pl.BlockSpec((seq, wide), lambda b, h, i: (b, km_col0 // hpb + h)),
            pl.BlockSpec((seq, LANES), lambda b, h, i: (b, 0)),
            pl.BlockSpec((seq, wide), lambda b, h, i: (b, v_col0 // hpb + h)),
            pl.BlockSpec(bias.shape, lambda b, h, i: (0, 0)),
        ] + cast_specs,
        out_specs=[pl.BlockSpec((tq, wide), lambda b, h, i: (q_row(b, i), h))] + cast_specs,
        out_shape=[jax.ShapeDtypeStruct((n, heads * LANES), BF16)]
        + [jax.ShapeDtypeStruct(arr.shape, BF16) for arr in cast_jobs],
        scratch_shapes=[pltpu.VMEM((hpb, seq, 2 * LANES), BF16), pltpu.VMEM((hpb, seq, 2 * LANES), BF16),
                        pltpu.VMEM((hpb, tq, LANES), F32), pltpu.VMEM((hpb, tq, 2 * LANES), F32)],
        compiler_params=_params("arbitrary", "arbitrary", "arbitrary"),
        name="attention",
    )(qm, qa, lane_mask, km, ka, v, bias, *cast_jobs)
    return out, copies


def _proj_ln_kernel(o_ref, w_ref, x_ref, g_ref, b_ref, y_ref, *, alpha):
    y = alpha * x_ref[...] + _dot(o_ref[...], w_ref[...])
    y_ref[...] = _layer_norm(y, g_ref[...], b_ref[...])


def _proj_ln(o, w, x, g, b, alpha, tm=512):
    n, d = x.shape
    row = lambda i: (i, 0)
    fixed = lambda i: (0, 0)
    return pl.pallas_call(
        functools.partial(_proj_ln_kernel, alpha=alpha),
        grid=(n // tm,),
        in_specs=[pl.BlockSpec((tm, o.shape[1]), row), pl.BlockSpec(w.shape, fixed),
                  pl.BlockSpec((tm, d), row), pl.BlockSpec((1, d), fixed), pl.BlockSpec((1, d), fixed)],
        out_specs=pl.BlockSpec((tm, d), row),
        out_shape=jax.ShapeDtypeStruct((n, d), F32),
        compiler_params=_params("arbitrary"),
        name="proj_ln",
    )(o, w, x, g, b)


def _ffn_kernel(x_ref, wg_ref, wu_ref, wd_ref, g_ref, b_ref, y_ref, x_bf, *, alpha):
    j = pl.program_id(1)

    @pl.when(j == 0)
    def _():
        x_bf[...] = x_ref[...].astype(BF16)
        y_ref[...] = jnp.zeros_like(y_ref)

    for r in range(0, x_bf.shape[0], FFN_ROW_CHUNK):
        rows = slice(r, r + FFN_ROW_CHUNK)
        xb = x_bf[rows, :]
        gate = _dot(xb, wg_ref[...])
        up = _dot(xb, wu_ref[...])
        y_ref[rows, :] += _dot((gate * jax.nn.sigmoid(gate) * up).astype(BF16), wd_ref[...])

    @pl.when(j == pl.num_programs(1) - 1)
    def _():
        for r in range(0, x_bf.shape[0], LN_ROW_CHUNK):
            rows = slice(r, r + LN_ROW_CHUNK)
            y_ref[rows, :] = _layer_norm(alpha * x_ref[rows, :] + y_ref[rows, :], g_ref[...], b_ref[...])


def _ffn(x, w_gu, w_down, layer, g, b, alpha, tm=1024, tf=512):
    n, d = x.shape
    hidden = w_down.shape[1]
    nf = hidden // tf
    return pl.pallas_call(
        functools.partial(_ffn_kernel, alpha=alpha),
        grid=(n // tm, nf),
        in_specs=[pl.BlockSpec((tm, d), lambda i, j: (i, 0)),
                  pl.BlockSpec((None, d, tf), lambda i, j: (layer, 0, j)),
                  pl.BlockSpec((None, d, tf), lambda i, j: (layer, 0, nf + j)),
                  pl.BlockSpec((None, tf, d), lambda i, j: (layer, j, 0)),
                  pl.BlockSpec((1, d), lambda i, j: (0, 0)), pl.BlockSpec((1, d), lambda i, j: (0, 0))],
        out_specs=pl.BlockSpec((tm, d), lambda i, j: (i, 0)),
        out_shape=jax.ShapeDtypeStruct((n, d), F32),
        scratch_shapes=[pltpu.VMEM((tm, d), BF16)],
        compiler_params=_params("arbitrary", "arbitrary"),
        name="ffn",
    )(x, w_gu, w_gu, w_down, g, b)


def _mla_weights(w_in, w_q_up, w_kv_up):
    d = w_in.shape[0]
    half = MLA_ROPE_DIM // 2
    base = MLA_Q_RANK + MLA_KV_RANK
    k1, k2 = w_in[:, base:base + half], w_in[:, base + half:]
    w_in_cat = jnp.concatenate([w_in[:, :base], k1, k2, k1, k2, -k2, k1, -k2, k1], axis=1)
    wq = w_q_up.reshape(MLA_Q_RANK, MLA_HEADS, MLA_NOPE_DIM + MLA_ROPE_DIM)
    r1 = wq[:, :, MLA_NOPE_DIM:MLA_NOPE_DIM + half]
    r2 = wq[:, :, MLA_NOPE_DIM + half:]
    w_q_cat = jnp.concatenate([
        wq[:, :, :MLA_NOPE_DIM].reshape(MLA_Q_RANK, -1),
        wq[:, :, MLA_NOPE_DIM:].reshape(MLA_Q_RANK, -1),
        jnp.concatenate([-r2, r1], axis=2).reshape(MLA_Q_RANK, -1)], axis=1)
    wkv = w_kv_up.reshape(MLA_KV_RANK, MLA_HEADS, MLA_NOPE_DIM + MLA_V_DIM)
    w_kv_cat = jnp.concatenate([wkv[:, :, :MLA_NOPE_DIM].reshape(MLA_KV_RANK, -1),
                                wkv[:, :, MLA_NOPE_DIM:].reshape(MLA_KV_RANK, -1)], axis=1)
    del d
    return w_in_cat.astype(BF16), w_q_cat.astype(BF16), w_kv_cat.astype(BF16)


def _cast_blocks(rows, steps):
    return max(b for b in range(1, steps + 1) if steps % b == 0 and rows % (16 * b) == 0)


def _causal_bias(tq, granularity):
    r = np.arange(tq)[:, None] // granularity
    c = np.arange(tq)[None, :] // granularity
    return jnp.asarray(np.where(c <= r, 0.0, NEG_INF), F32)


def _mla_lane_mask():
    lane = np.arange(LANES)[None, :]
    head = np.arange(MLA_HEADS)[:, None]
    m = (lane // MLA_ROPE_DIM == head % 2).astype(np.float32)
    return jnp.asarray(np.broadcast_to(m[:, None, :], (MLA_HEADS, 8, LANES)), BF16)


def _fox_lane_mask():
    lane = np.arange(LANES)[None, :]
    head = np.arange(FOX_HEADS)[:, None]
    m = ((lane % FOX_HEADS == head) & (lane < 6 * FOX_HEADS)).astype(np.float32)
    return jnp.asarray(np.broadcast_to(m[:, None, :], (FOX_HEADS, 8, LANES)), BF16)


def _mla_layer(x, pos, w_in, q_norm_g, w_q_up, kv_norm_g, w_kv_up, batch, cast_jobs):
    w_in_cat, w_q_cat, w_kv_cat = _mla_weights(w_in, w_q_up, w_kv_up)
    half = MLA_ROPE_DIM // 2
    inv_freq = ROPE_THETA ** (-jnp.arange(0, MLA_ROPE_DIM, 2, dtype=F32) / MLA_ROPE_DIM)
    freq = jnp.tile(inv_freq, LANES // half)[None, :]
    scale = (MLA_NOPE_DIM + MLA_ROPE_DIM) ** -0.5 * LOG2E
    qn, qr, kv, kr = _mla_proj(x, pos, freq, w_in_cat, q_norm_g[None, :], kv_norm_g[None, :],
                               w_q_cat, w_kv_cat, scale)
    o, copies = _attention(qn, 0, qr, ATTN_HEADS_PER_STEP // 2, _mla_lane_mask(), kv, 0, kr, kv, MLA_HEADS,
                           CHUNK, batch, MLA_HEADS, cast_jobs)
    return o, copies


def _fox_layer(x, w_bf, b_f, batch, cast_jobs):
    width = FOX_HEADS * FOX_HEAD_DIM
    qkv = _matmul(x, w_bf, 3 * width, tm=1024, tn=width,
                  n_scaled=1, scale=FOX_HEAD_DIM ** -0.5 * LOG2E)
    groups = 6
    w_f = jnp.pad(jnp.tile(w_bf[:, 3 * width:], (1, groups)), ((0, 0), (0, LANES - groups * FOX_HEADS)))
    b6 = jnp.pad(jnp.tile(b_f, groups), (0, LANES - groups * FOX_HEADS))[None, :]
    aq, ak = _fox_gate(x, w_f, b6, batch)
    o, copies = _attention(qkv, 0, aq, 0, _fox_lane_mask(), qkv, FOX_HEADS, ak, qkv, 2 * FOX_HEADS,
                           1, batch, FOX_HEADS, cast_jobs)
    return o, copies


def kernel(x, positions, mla_w_in, mla_q_norm_g, mla_w_q_up, mla_kv_norm_g, mla_w_kv_up, mla_w_o,
           fox_w_in, fox_b_f, fox_w_o, ffn_w_gu, ffn_w_down, ln_mix_g, ln_mix_b, ln_ffn_g, ln_ffn_b):
    batch, seq, d = x.shape
    depth = ffn_w_gu.shape[0]
    alpha = float((2 * depth) ** 0.25)
    h = x.reshape(batch * seq, d)
    pos = positions.astype(F32).reshape(batch * seq, 1)
    late = [ffn_w_gu, ffn_w_down, mla_w_o, fox_w_o, fox_w_in]
    cast_jobs = [w.reshape(-1, w.shape[2]) for w in late]
    for i in range(depth):
        j = i // 2
        if i % 2 == 0:
            o, copies = _mla_layer(h, pos, mla_w_in[j], mla_q_norm_g[j], mla_w_q_up[j], mla_kv_norm_g[j],
                                   mla_w_kv_up[j], batch, cast_jobs)
            if cast_jobs:
                w_gu_bf, w_down_bf, mla_w_o_bf, fox_w_o_bf, fox_w_in_bf = [
                    c.reshape(w.shape) for c, w in zip(copies, late)]
                cast_jobs = []
            w_o = mla_w_o_bf[j]
        else:
            o, _ = _fox_layer(h, fox_w_in_bf[j], fox_b_f[j], batch, cast_jobs)
            w_o = fox_w_o_bf[j]
        h = _proj_ln(o, w_o, h, ln_mix_g[i][None, :], ln_mix_b[i][None, :], alpha)
        h = _ffn(h, w_gu_bf, w_down_bf, i, ln_ffn_g[i][None, :], ln_ffn_b[i][None, :], alpha)
    return h.reshape(batch, seq, d)
```

```python
import functools

import jax
import jax.numpy as jnp
import numpy as np
from jax import lax
from jax.experimental import pallas as pl
from jax.experimental.pallas import tpu as pltpu

CHUNK = 64
MLA_HEADS = 16
MLA_NOPE_DIM = 128
MLA_ROPE_DIM = 64
MLA_V_DIM = 128
MLA_Q_RANK = 512
MLA_KV_RANK = 512
ROPE_THETA = 10000.0
FOX_HEADS = 16
FOX_HEAD_DIM = 128
LN_EPS = 1e-5
RMS_EPS = 1e-6
NEG_INF = -1e30
LOG2E = 1.4426950408889634

LANES = 128
VMEM_LIMIT_BYTES = 62 * 1024 * 1024
ATTN_HEADS_PER_STEP = 4
ATTN_TILES_PER_BLOCK = 4
FFN_ROW_CHUNK = 512
LN_ROW_CHUNK = 256
MLA_KV_COL_CHUNK = 1024
PROJ_ROW_CHUNK = 128

BF16 = jnp.bfloat16
F32 = jnp.float32


def _params(*semantics):
    return pltpu.CompilerParams(dimension_semantics=semantics, vmem_limit_bytes=VMEM_LIMIT_BYTES)


def _dot(a, b):
    return jnp.dot(a, b, preferred_element_type=F32)


def _layer_norm(y, g, b):
    mu = jnp.mean(y, axis=-1, keepdims=True)
    d = y - mu
    var = jnp.mean(d * d, axis=-1, keepdims=True)
    return d * lax.rsqrt(var + LN_EPS) * g + b


def _rms_norm(y, g):
    return y * lax.rsqrt(jnp.mean(y * y, axis=-1, keepdims=True) + RMS_EPS) * g


def _mla_proj_kernel(x_ref, pos_ref, freq_ref, w_in_ref, gq_ref, gkv_ref, wq_ref, wkv_ref,
                     qn_ref, qr_ref, kv_ref, kr_ref, *, scale):
    n_nope = MLA_HEADS * MLA_NOPE_DIM
    n_rope = MLA_HEADS * MLA_ROPE_DIM
    base = MLA_Q_RANK + MLA_KV_RANK
    h = _dot(x_ref[...].astype(BF16), w_in_ref[...])
    cq = _rms_norm(h[:, :MLA_Q_RANK], gq_ref[...]).astype(BF16)
    ckv = _rms_norm(h[:, MLA_Q_RANK:base], gkv_ref[...]).astype(BF16)
    ang = pos_ref[...] * freq_ref[...]
    c = jnp.cos(ang)
    s = jnp.sin(ang)
    kr_ref[...] = (h[:, base:base + LANES] * c + h[:, base + LANES:] * s).astype(BF16)
    qn_ref[...] = (_dot(cq, wq_ref[:, :n_nope]) * scale).astype(BF16)
    r = _dot(cq, wq_ref[:, n_nope:n_nope + n_rope])
    rp = _dot(cq, wq_ref[:, n_nope + n_rope:])
    for p in range(n_rope // LANES):
        sl = slice(p * LANES, (p + 1) * LANES)
        qr_ref[:, sl] = ((r[:, sl] * c + rp[:, sl] * s) * scale).astype(BF16)
    for col in range(0, kv_ref.shape[1], MLA_KV_COL_CHUNK):
        cols = slice(col, col + MLA_KV_COL_CHUNK)
        kv_ref[:, cols] = _dot(ckv, wkv_ref[:, cols]).astype(BF16)


def _mla_proj(x, pos, freq, w_in, gq, gkv, wq, wkv, scale, tm=512):
    n, d = x.shape
    n_nope = MLA_HEADS * MLA_NOPE_DIM
    n_rope = MLA_HEADS * MLA_ROPE_DIM
    row = lambda i: (i, 0)
    fixed = lambda i: (0, 0)
    resident = lambda w: pl.BlockSpec(w.shape, fixed, pipeline_mode=pl.Buffered(1))
    return pl.pallas_call(
        functools.partial(_mla_proj_kernel, scale=scale),
        grid=(n // tm,),
        in_specs=[pl.BlockSpec((tm, d), row), pl.BlockSpec((tm, 1), row), pl.BlockSpec((1, LANES), fixed),
                  resident(w_in), pl.BlockSpec((1, MLA_Q_RANK), fixed), pl.BlockSpec((1, MLA_KV_RANK), fixed),
                  resident(wq), resident(wkv)],
        out_specs=[pl.BlockSpec((tm, n_nope), row), pl.BlockSpec((tm, n_rope), row),
                   pl.BlockSpec((tm, wkv.shape[1]), row), pl.BlockSpec((tm, LANES), row)],
        out_shape=[jax.ShapeDtypeStruct((n, n_nope), BF16), jax.ShapeDtypeStruct((n, n_rope), BF16),
                   jax.ShapeDtypeStruct((n, wkv.shape[1]), BF16), jax.ShapeDtypeStruct((n, LANES), BF16)],
        compiler_params=_params("arbitrary"),
        name="mla_proj",
    )(x, pos, freq, w_in, gq, gkv, wq, wkv)


def _matmul_kernel(a_ref, b_ref, o_ref, a_bf, *, n_scaled, scale):
    j = pl.program_id(1)

    @pl.when(j == 0)
    def _():
        a_bf[...] = a_ref[...].astype(BF16)

    acc = lax.dot_general(a_bf[...], b_ref[...], (((1,), (1,)), ((), ())), preferred_element_type=F32)
    if n_scaled:
        acc = acc * jnp.where(j < n_scaled, scale, 1.0).astype(F32)
    o_ref[...] = acc.astype(o_ref.dtype)


def _matmul_nt(a, b_t, n, tm, tn, n_scaled=0, scale=1.0):
    m, k = a.shape
    return pl.pallas_call(
        functools.partial(_matmul_kernel, n_scaled=n_scaled, scale=scale),
        grid=(m // tm, n // tn),
        in_specs=[pl.BlockSpec((tm, k), lambda i, j: (i, 0)), pl.BlockSpec((tn, k), lambda i, j: (j, 0))],
        out_specs=pl.BlockSpec((tm, tn), lambda i, j: (i, j)),
        out_shape=jax.ShapeDtypeStruct((m, n), BF16),
        scratch_shapes=[pltpu.VMEM((tm, k), BF16)],
        compiler_params=_params("arbitrary", "arbitrary"),
        name="matmul_nt",
    )(a, b_t)


def _split3(v):
    hi = v.astype(BF16).astype(F32)
    r1 = v - hi
    mid = r1.astype(BF16).astype(F32)
    lo = (r1 - mid).astype(BF16).astype(F32)
    return hi, mid, lo


def _fox_gate_kernel(x_ref, w_ref, b_ref, tri_ref, aq_ref, ak_ref, carry):
    t = pl.program_id(1)

    @pl.when(t == 0)
    def _():
        carry[...] = jnp.zeros_like(carry)

    logit = _dot(x_ref[...].astype(BF16), w_ref[...]) + b_ref[...]
    log_f = jnp.minimum(logit, 0.0) - jnp.log1p(jnp.exp(-jnp.abs(logit)))
    tri = tri_ref[...]
    hi, mid, lo = _split3(log_f)
    c = (_dot(tri, hi.astype(BF16)) + _dot(tri, mid.astype(BF16))) + _dot(tri, lo.astype(BF16))
    c = c + carry[0:1, :]
    ts = c.shape[0]
    carry[0:1, :] = c[ts - 1:ts, :]
    hi, mid, lo = _split3(c * LOG2E)
    group = lax.broadcasted_iota(jnp.int32, c.shape, 1) // FOX_HEADS
    pieces = jnp.where(group % 3 == 0, hi, jnp.where(group % 3 == 1, mid, lo))
    aq_ref[...] = jnp.where(group < 3, pieces, jnp.where(group < 6, 1.0, 0.0)).astype(BF16)
    ak_ref[...] = jnp.where(group < 3, 1.0, jnp.where(group < 6, -pieces, 0.0)).astype(BF16)


def _fox_gate(x, w, b, batch, ts=512):
    n, d = x.shape
    steps = n // batch // ts
    tri = jnp.asarray(np.tril(np.ones((ts, ts), np.float32)), BF16)
    row = lambda bi, t: (bi * steps + t, 0)
    fixed = lambda bi, t: (0, 0)
    return pl.pallas_call(
        _fox_gate_kernel,
        grid=(batch, steps),
        in_specs=[pl.BlockSpec((ts, d), row), pl.BlockSpec(w.shape, fixed),
                  pl.BlockSpec((1, LANES), fixed), pl.BlockSpec((ts, ts), fixed)],
        out_specs=[pl.BlockSpec((ts, LANES), row), pl.BlockSpec((ts, LANES), row)],
        out_shape=[jax.ShapeDtypeStruct((n, LANES), BF16), jax.ShapeDtypeStruct((n, LANES), BF16)],
        scratch_shapes=[pltpu.VMEM((8, LANES), F32)],
        compiler_params=_params("arbitrary", "arbitrary"),
        name="fox_gate",
    )(x, w, b, tri)


def _attn_kernel(qm_ref, qa_ref, lm_ref, km_ref, ka_ref, v_ref, bias_ref, *rest, tq, hpb, n_cast):
    cast_in, (o_ref, *cast_out), (kcat, vcat, m_sc, acc_sc) = rest[:n_cast], rest[n_cast:2 * n_cast + 1], rest[2 * n_cast + 1:]
    i = pl.program_id(2)
    for src, dst in zip(cast_in, cast_out):
        dst[...] = src[...].astype(BF16)

    @pl.when(i == 0)
    def _():
        ones = jnp.ones(ka_ref.shape, BF16)
        for g in range(hpb):
            head = slice(g * LANES, (g + 1) * LANES)
            kcat[g, :, :LANES] = km_ref[:, head]
            kcat[g, :, LANES:] = ka_ref[...]
            vcat[g, :, :LANES] = v_ref[:, head]
            vcat[g, :, LANES:] = ones

    qa_tiles = qa_ref.shape[1] // LANES

    def q_tile(g):
        t = g * qa_tiles // hpb
        aux = qa_ref[:, t * LANES:(t + 1) * LANES] * lm_ref[g, 0:1, :]
        return jnp.concatenate([qm_ref[:, g * LANES:(g + 1) * LANES], aux], axis=1)

    q = [q_tile(g) for g in range(hpb)]

    nt = (((1,), (1,)), ((), ()))
    half = tq // 2

    def row_max(s):
        return jnp.broadcast_to(jnp.max(s, axis=-1, keepdims=True), (s.shape[0], LANES))

    def tile(g, j):
        rows = pl.ds(pl.multiple_of(j * tq, tq), tq)
        s = lax.dot_general(q[g], kcat[g, rows, :], nt, preferred_element_type=F32)
        m_new = jnp.maximum(m_sc[g], row_max(s))
        p = jnp.exp2(s - jnp.tile(m_new, (1, tq // LANES))).astype(BF16)
        pv = _dot(p, vcat[g, rows, :])
        acc_sc[g] = jnp.tile(jnp.exp2(m_sc[g] - m_new), (1, 2)) * acc_sc[g] + pv
        m_sc[g] = m_new

    def diagonal_tile(g, tri):
        left = pl.ds(pl.multiple_of(i * tq, tq), half)
        right = pl.ds(pl.multiple_of(i * tq + half, half), half)
        s_l = lax.dot_general(q[g], kcat[g, left, :], nt, preferred_element_type=F32)
        s_l = jnp.concatenate([s_l[:half] + tri, s_l[half:]], axis=0)
        s_r = lax.dot_general(q[g][half:], kcat[g, right, :], nt, preferred_element_type=F32) + tri
        m_top = row_max(s_l[:half])
        m_bot = jnp.maximum(row_max(s_l[half:]), row_max(s_r))
        m_new = jnp.concatenate([m_top, m_bot], axis=0)
        p_l = jnp.exp2(s_l - jnp.tile(m_new, (1, half // LANES))).astype(BF16)
        p_r = jnp.exp2(s_r - jnp.tile(m_bot, (1, half // LANES))).astype(BF16)
        pv_l = _dot(p_l, vcat[g, left, :])
        acc_sc[g, :half] = pv_l[:half]
        acc_sc[g, half:] = pv_l[half:] + _dot(p_r, vcat[g, right, :])
        m_sc[g] = m_new

    def tiles(js, with_diagonal=False):
        if with_diagonal:
            tri = bias_ref[...]
            for g in range(hpb):
                diagonal_tile(g, tri)
        for j in js:
            for g in range(hpb):
                tile(g, j)

    group = ATTN_TILES_PER_BLOCK
    rem = i % group
    for r in range(group):
        @pl.when(rem == r)
        def _(r=r):
            tiles([i - t for t in range(1, r + 1)], with_diagonal=True)

    def body(jj, carry):
        tiles([group * jj + t for t in range(group)])
        return carry

    lax.fori_loop(0, i // group, body, 0)
    for g in range(hpb):
        acc = acc_sc[g]
        o_ref[:, g * LANES:(g + 1) * LANES] = (acc[:, :LANES] / acc[:, LANES:]).astype(o_ref.dtype)


def _attention(qm, qm_col0, qa, qa_tiles, lane_mask, km, km_col0, ka, v, v_col0, granularity, batch, heads,
               cast_jobs=(), tq=512, hpb=ATTN_HEADS_PER_STEP):
    n = qm.shape[0]
    seq = n // batch
    nq = seq // tq
    wide = hpb * LANES
    head_blocks = heads // hpb
    steps = batch * head_blocks * nq
    q_row = lambda b, i: b * nq + i
    bias = _causal_bias(tq // 2, granularity)

    def cast_spec(arr):
        blocks = _cast_blocks(arr.shape[0], steps)
        return pl.BlockSpec((arr.shape[0] // blocks, arr.shape[1]),
                            lambda b, h, i: (jnp.minimum((b * head_blocks + h) * nq + i, blocks - 1), 0))

    cast_specs = [cast_spec(arr) for arr in cast_jobs]
    out, *copies = pl.pallas_call(
        functools.partial(_attn_kernel, tq=tq, hpb=hpb, n_cast=len(cast_jobs)),
        grid=(batch, head_blocks, nq),
        in_specs=[
            pl.BlockSpec((tq, wide), lambda b, h, i: (q_row(b, i), qm_col0 // hpb + h)),
            pl.BlockSpec((tq, max(qa_tiles, 1) * LANES), lambda b, h, i: (q_row(b, i), h if qa_tiles else 0)),
            pl.BlockSpec((hpb, 8, LANES), lambda b, h, i: (h, 0, 0)),
            pl.BlockSpec((seq, wide), lambda b, h, i: (b, km_col0 // hpb + h)),
            pl.BlockSpec((seq, LANES), lambda b, h, i: (b, 0)),
            pl.BlockSpec((seq, wide), lambda b, h, i: (b, v_col0 // hpb + h)),
            pl.BlockSpec(bias.shape, lambda b, h, i: (0, 0)),
        ] + cast_specs,
        out_specs=[pl.BlockSpec((tq, wide), lambda b, h, i: (q_row(b, i), h))] + cast_specs,
        out_shape=[jax.ShapeDtypeStruct((n, heads * LANES), BF16)]
        + [jax.ShapeDtypeStruct(arr.shape, BF16) for arr in cast_jobs],
        scratch_shapes=[pltpu.VMEM((hpb, seq, 2 * LANES), BF16), pltpu.VMEM((hpb, seq, 2 * LANES), BF16),
                        pltpu.VMEM((hpb, tq, LANES), F32), pltpu.VMEM((hpb, tq, 2 * LANES), F32)],
        compiler_params=_params("arbitrary", "arbitrary", "arbitrary"),
        name="attention",
    )(qm, qa, lane_mask, km, ka, v, bias, *cast_jobs)
    return out, copies


def _proj_ln_kernel(o_ref, w_ref, x_ref, g_ref, b_ref, y_ref, *, alpha):
    for r in range(0, x_ref.shape[0], PROJ_ROW_CHUNK):
        rows = slice(r, r + PROJ_ROW_CHUNK)
        y = alpha * x_ref[rows, :] + _dot(o_ref[rows, :], w_ref[...])
        y_ref[rows, :] = _layer_norm(y, g_ref[...], b_ref[...])


def _proj_ln(o, w, x, g, b, alpha, tm=512):
    n, d = x.shape
    row = lambda i: (i, 0)
    fixed = lambda i: (0, 0)
    return pl.pallas_call(
        functools.partial(_proj_ln_kernel, alpha=alpha),
        grid=(n // tm,),
        in_specs=[pl.BlockSpec((tm, o.shape[1]), row), pl.BlockSpec(w.shape, fixed),
                  pl.BlockSpec((tm, d), row), pl.BlockSpec((1, d), fixed), pl.BlockSpec((1, d), fixed)],
        out_specs=pl.BlockSpec((tm, d), row),
        out_shape=jax.ShapeDtypeStruct((n, d), F32),
        compiler_params=_params("arbitrary"),
        name="proj_ln",
    )(o, w, x, g, b)


def _ffn_kernel(x_ref, wg_ref, wu_ref, wd_ref, g_ref, b_ref, y_ref, x_bf, *, alpha):
    j = pl.program_id(1)

    @pl.when(j == 0)
    def _():
        x_bf[...] = x_ref[...].astype(BF16)
        y_ref[...] = jnp.zeros_like(y_ref)

    for r in range(0, x_bf.shape[0], FFN_ROW_CHUNK):
        rows = slice(r, r + FFN_ROW_CHUNK)
        xb = x_bf[rows, :]
        gate = _dot(xb, wg_ref[...])
        up = _dot(xb, wu_ref[...])
        y_ref[rows, :] += _dot((gate * jax.nn.sigmoid(gate) * up).astype(BF16), wd_ref[...])

    @pl.when(j == pl.num_programs(1) - 1)
    def _():
        for r in range(0, x_bf.shape[0], LN_ROW_CHUNK):
            rows = slice(r, r + LN_ROW_CHUNK)
            y_ref[rows, :] = _layer_norm(alpha * x_ref[rows, :] + y_ref[rows, :], g_ref[...], b_ref[...])


def _ffn(x, w_gu, w_down, layer, g, b, alpha, tm=1024, tf=512):
    n, d = x.shape
    hidden = w_down.shape[1]
    nf = hidden // tf
    return pl.pallas_call(
        functools.partial(_ffn_kernel, alpha=alpha),
        grid=(n // tm, nf),
        in_specs=[pl.BlockSpec((tm, d), lambda i, j: (i, 0)),
                  pl.BlockSpec((None, d, tf), lambda i, j: (layer, 0, j)),
                  pl.BlockSpec((None, d, tf), lambda i, j: (layer, 0, nf + j)),
                  pl.BlockSpec((None, tf, d), lambda i, j: (layer, j, 0)),
                  pl.BlockSpec((1, d), lambda i, j: (0, 0)), pl.BlockSpec((1, d), lambda i, j: (0, 0))],
        out_specs=pl.BlockSpec((tm, d), lambda i, j: (i, 0)),
        out_shape=jax.ShapeDtypeStruct((n, d), F32),
        scratch_shapes=[pltpu.VMEM((tm, d), BF16)],
        compiler_params=_params("arbitrary", "arbitrary"),
        name="ffn",
    )(x, w_gu, w_gu, w_down, g, b)


def _mla_weights(w_in, w_q_up, w_kv_up):
    d = w_in.shape[0]
    half = MLA_ROPE_DIM // 2
    base = MLA_Q_RANK + MLA_KV_RANK
    k1, k2 = w_in[:, base:base + half], w_in[:, base + half:]
    w_in_cat = jnp.concatenate([w_in[:, :base], k1, k2, k1, k2, -k2, k1, -k2, k1], axis=1)
    wq = w_q_up.reshape(MLA_Q_RANK, MLA_HEADS, MLA_NOPE_DIM + MLA_ROPE_DIM)
    r1 = wq[:, :, MLA_NOPE_DIM:MLA_NOPE_DIM + half]
    r2 = wq[:, :, MLA_NOPE_DIM + half:]
    w_q_cat = jnp.concatenate([
        wq[:, :, :MLA_NOPE_DIM].reshape(MLA_Q_RANK, -1),
        wq[:, :, MLA_NOPE_DIM:].reshape(MLA_Q_RANK, -1),
        jnp.concatenate([-r2, r1], axis=2).reshape(MLA_Q_RANK, -1)], axis=1)
    wkv = w_kv_up.reshape(MLA_KV_RANK, MLA_HEADS, MLA_NOPE_DIM + MLA_V_DIM)
    w_kv_cat = jnp.concatenate([wkv[:, :, :MLA_NOPE_DIM].reshape(MLA_KV_RANK, -1),
                                wkv[:, :, MLA_NOPE_DIM:].reshape(MLA_KV_RANK, -1)], axis=1)
    del d
    return w_in_cat.astype(BF16), w_q_cat.astype(BF16), w_kv_cat.astype(BF16)


def _cast_blocks(rows, steps):
    return max(b for b in range(1, steps + 1) if rows % (16 * b) == 0)


def _causal_bias(tq, granularity):
    r = np.arange(tq)[:, None] // granularity
    c = np.arange(tq)[None, :] // granularity
    return jnp.asarray(np.where(c <= r, 0.0, NEG_INF), F32)


def _mla_lane_mask():
    lane = np.arange(LANES)[None, :]
    head = np.arange(MLA_HEADS)[:, None]
    m = (lane // MLA_ROPE_DIM == head % 2).astype(np.float32)
    return jnp.asarray(np.broadcast_to(m[:, None, :], (MLA_HEADS, 8, LANES)), BF16)


def _fox_lane_mask():
    lane = np.arange(LANES)[None, :]
    head = np.arange(FOX_HEADS)[:, None]
    m = ((lane % FOX_HEADS == head) & (lane < 6 * FOX_HEADS)).astype(np.float32)
    return jnp.asarray(np.broadcast_to(m[:, None, :], (FOX_HEADS, 8, LANES)), BF16)


def _mla_layer(x, pos, w_in, q_norm_g, w_q_up, kv_norm_g, w_kv_up, batch, cast_jobs):
    w_in_cat, w_q_cat, w_kv_cat = _mla_weights(w_in, w_q_up, w_kv_up)
    half = MLA_ROPE_DIM // 2
    inv_freq = ROPE_THETA ** (-jnp.arange(0, MLA_ROPE_DIM, 2, dtype=F32) / MLA_ROPE_DIM)
    freq = jnp.tile(inv_freq, LANES // half)[None, :]
    scale = (MLA_NOPE_DIM + MLA_ROPE_DIM) ** -0.5 * LOG2E
    qn, qr, kv, kr = _mla_proj(x, pos, freq, w_in_cat, q_norm_g[None, :], kv_norm_g[None, :],
                               w_q_cat, w_kv_cat, scale)
    o, copies = _attention(qn, 0, qr, ATTN_HEADS_PER_STEP // 2, _mla_lane_mask(), kv, 0, kr, kv, MLA_HEADS,
                           CHUNK, batch, MLA_HEADS, cast_jobs)
    return o, copies


def _fox_layer(x, w_t, b_f, batch, cast_jobs):
    width = FOX_HEADS * FOX_HEAD_DIM
    qkv = _matmul_nt(x, w_t, 3 * width, tm=1024, tn=width,
                     n_scaled=1, scale=FOX_HEAD_DIM ** -0.5 * LOG2E)
    groups = 6
    w_f = jnp.pad(jnp.tile(w_t[3 * width:].T, (1, groups)), ((0, 0), (0, LANES - groups * FOX_HEADS)))
    b6 = jnp.pad(jnp.tile(b_f, groups), (0, LANES - groups * FOX_HEADS))[None, :]
    aq, ak = _fox_gate(x, w_f, b6, batch)
    o, copies = _attention(qkv, 0, aq, 0, _fox_lane_mask(), qkv, FOX_HEADS, ak, qkv, 2 * FOX_HEADS,
                           1, batch, FOX_HEADS, cast_jobs)
    return o, copies


def kernel(x, positions, mla_w_in, mla_q_norm_g, mla_w_q_up, mla_kv_norm_g, mla_w_kv_up, mla_w_o,
           fox_w_in, fox_b_f, fox_w_o, ffn_w_gu, ffn_w_down, ln_mix_g, ln_mix_b, ln_ffn_g, ln_ffn_b):
    batch, seq, d = x.shape
    depth = ffn_w_gu.shape[0]
    alpha = float((2 * depth) ** 0.25)
    h = x.reshape(batch * seq, d)
    pos = positions.astype(F32).reshape(batch * seq, 1)
    late = [ffn_w_gu, ffn_w_down, mla_w_o, fox_w_o, jnp.swapaxes(fox_w_in, 1, 2)]
    cast_jobs = [w.reshape(-1, w.shape[2]) for w in late]
    for i in range(depth):
        j = i // 2
        if i % 2 == 0:
            o, copies = _mla_layer(h, pos, mla_w_in[j], mla_q_norm_g[j], mla_w_q_up[j], mla_kv_norm_g[j],
                                   mla_w_kv_up[j], batch, cast_jobs)
            if cast_jobs:
                w_gu_bf, w_down_bf, mla_w_o_bf, fox_w_o_bf, fox_w_in_t_bf = [
                    c.reshape(w.shape) for c, w in zip(copies, late)]
                cast_jobs = []
            w_o = mla_w_o_bf[j]
        else:
            o, _ = _fox_layer(h, fox_w_in_t_bf[j], fox_b_f[j], batch, cast_jobs)
            w_o = fox_w_o_bf[j]
        h = _proj_ln(o, w_o, h, ln_mix_g[i][None, :], ln_mix_b[i][None, :], alpha)
        h = _ffn(h, w_gu_bf, w_down_bf, i, ln_ffn_g[i][None, :], ln_ffn_b[i][None, :], alpha)
    return h.reshape(batch, seq, d)
```

```python
import functools

import jax
import jax.numpy as jnp
import numpy as np
from jax import lax
from jax.experimental import pallas as pl
from jax.experimental.pallas import tpu as pltpu

CHUNK = 64
MLA_HEADS = 16
MLA_NOPE_DIM = 128
MLA_ROPE_DIM = 64
MLA_V_DIM = 128
MLA_Q_RANK = 512
MLA_KV_RANK = 512
ROPE_THETA = 10000.0
FOX_HEADS = 16
FOX_HEAD_DIM = 128
LN_EPS = 1e-5
RMS_EPS = 1e-6
NEG_INF = -1e30
LOG2E = 1.4426950408889634

LANES = 128
VMEM_LIMIT_BYTES = 62 * 1024 * 1024
ATTN_HEADS_PER_STEP = 4
ATTN_TILES_PER_BLOCK = 4
FFN_ROW_CHUNK = 512
LN_ROW_CHUNK = 256
MLA_KV_COL_CHUNK = 1024
PROJ_ROW_CHUNK = 128

BF16 = jnp.bfloat16
F32 = jnp.float32


def _params(*semantics):
    return pltpu.CompilerParams(dimension_semantics=semantics, vmem_limit_bytes=VMEM_LIMIT_BYTES)


def _dot(a, b):
    return jnp.dot(a, b, preferred_element_type=F32)


def _layer_norm(y, g, b):
    mu = jnp.mean(y, axis=-1, keepdims=True)
    d = y - mu
    var = jnp.mean(d * d, axis=-1, keepdims=True)
    return d * lax.rsqrt(var + LN_EPS) * g + b


def _rms_norm(y, g):
    return y * lax.rsqrt(jnp.mean(y * y, axis=-1, keepdims=True) + RMS_EPS) * g


def _mla_proj_kernel(x_ref, pos_ref, freq_ref, w_in_ref, gq_ref, gkv_ref, wq_ref, wkv_ref,
                     qn_ref, qr_ref, kv_ref, kr_ref, *, scale):
    n_nope = MLA_HEADS * MLA_NOPE_DIM
    n_rope = MLA_HEADS * MLA_ROPE_DIM
    base = MLA_Q_RANK + MLA_KV_RANK
    h = _dot(x_ref[...].astype(BF16), w_in_ref[...])
    cq = _rms_norm(h[:, :MLA_Q_RANK], gq_ref[...]).astype(BF16)
    ckv = _rms_norm(h[:, MLA_Q_RANK:base], gkv_ref[...]).astype(BF16)
    ang = pos_ref[...] * freq_ref[...]
    c = jnp.cos(ang)
    s = jnp.sin(ang)
    kr_ref[...] = (h[:, base:base + LANES] * c + h[:, base + LANES:] * s).astype(BF16)
    qn_ref[...] = (_dot(cq, wq_ref[:, :n_nope]) * scale).astype(BF16)
    r = _dot(cq, wq_ref[:, n_nope:n_nope + n_rope])
    rp = _dot(cq, wq_ref[:, n_nope + n_rope:])
    for p in range(n_rope // LANES):
        sl = slice(p * LANES, (p + 1) * LANES)
        qr_ref[:, sl] = ((r[:, sl] * c + rp[:, sl] * s) * scale).astype(BF16)
    for col in range(0, kv_ref.shape[1], MLA_KV_COL_CHUNK):
        cols = slice(col, col + MLA_KV_COL_CHUNK)
        kv_ref[:, cols] = _dot(ckv, wkv_ref[:, cols]).astype(BF16)


def _mla_proj(x, pos, freq, w_in, gq, gkv, wq, wkv, scale, tm=512):
    n, d = x.shape
    n_nope = MLA_HEADS * MLA_NOPE_DIM
    n_rope = MLA_HEADS * MLA_ROPE_DIM
    row = lambda i: (i, 0)
    fixed = lambda i: (0, 0)
    resident = lambda w: pl.BlockSpec(w.shape, fixed, pipeline_mode=pl.Buffered(1))
    return pl.pallas_call(
        functools.partial(_mla_proj_kernel, scale=scale),
        grid=(n // tm,),
        in_specs=[pl.BlockSpec((tm, d), row), pl.BlockSpec((tm, 1), row), pl.BlockSpec((1, LANES), fixed),
                  resident(w_in), pl.BlockSpec((1, MLA_Q_RANK), fixed), pl.BlockSpec((1, MLA_KV_RANK), fixed),
                  resident(wq), resident(wkv)],
        out_specs=[pl.BlockSpec((tm, n_nope), row), pl.BlockSpec((tm, n_rope), row),
                   pl.BlockSpec((tm, wkv.shape[1]), row), pl.BlockSpec((tm, LANES), row)],
        out_shape=[jax.ShapeDtypeStruct((n, n_nope), BF16), jax.ShapeDtypeStruct((n, n_rope), BF16),
                   jax.ShapeDtypeStruct((n, wkv.shape[1]), BF16), jax.ShapeDtypeStruct((n, LANES), BF16)],
        compiler_params=_params("arbitrary"),
        name="mla_proj",
    )(x, pos, freq, w_in, gq, gkv, wq, wkv)


def _split3(v):
    hi = v.astype(BF16).astype(F32)
    r1 = v - hi
    mid = r1.astype(BF16).astype(F32)
    lo = (r1 - mid).astype(BF16).astype(F32)
    return hi, mid, lo


def _fox_in_kernel(x_ref, wt_ref, wf_ref, bf_ref, tri_ref, o_ref, aq_ref, ak_ref, x_bf, carry,
                   *, scale, tiles_per_seq):
    i = pl.program_id(0)
    j = pl.program_id(1)

    @pl.when(j == 0)
    def _():
        x_bf[...] = x_ref[...].astype(BF16)

        @pl.when(i % tiles_per_seq == 0)
        def _():
            carry[...] = jnp.zeros_like(carry)

        logit = _dot(x_bf[...], wf_ref[...]) + bf_ref[...]
        log_f = jnp.minimum(logit, 0.0) - jnp.log1p(jnp.exp(-jnp.abs(logit)))
        tri = tri_ref[...]
        ts = tri.shape[0]
        group = lax.broadcasted_iota(jnp.int32, (ts, LANES), 1) // FOX_HEADS
        for r in range(0, x_bf.shape[0], ts):
            hi, mid, lo = _split3(log_f[r:r + ts])
            c = (_dot(tri, hi.astype(BF16)) + _dot(tri, mid.astype(BF16))) + _dot(tri, lo.astype(BF16))
            c = c + carry[0:1, :]
            carry[0:1, :] = c[ts - 1:ts, :]
            hi, mid, lo = _split3(c * LOG2E)
            pieces = jnp.where(group % 3 == 0, hi, jnp.where(group % 3 == 1, mid, lo))
            aq_ref[r:r + ts, :] = jnp.where(group < 3, pieces, jnp.where(group < 6, 1.0, 0.0)).astype(BF16)
            ak_ref[r:r + ts, :] = jnp.where(group < 3, 1.0, jnp.where(group < 6, -pieces, 0.0)).astype(BF16)

    acc = lax.dot_general(x_bf[...], wt_ref[...], (((1,), (1,)), ((), ())), preferred_element_type=F32)
    o_ref[...] = (acc * jnp.where(j == 0, scale, 1.0).astype(F32)).astype(o_ref.dtype)


def _fox_in(x, w_t, w_f, b_f, batch, scale, tm=1024, ts=512):
    m, k = x.shape
    width = FOX_HEADS * FOX_HEAD_DIM
    tri = jnp.asarray(np.tril(np.ones((ts, ts), np.float32)), BF16)
    row = lambda i, j: (i, 0)
    fixed = lambda i, j: (0, 0)
    return pl.pallas_call(
        functools.partial(_fox_in_kernel, scale=scale, tiles_per_seq=m // batch // tm),
        grid=(m // tm, 3),
        in_specs=[pl.BlockSpec((tm, k), row), pl.BlockSpec((width, k), lambda i, j: (j, 0)),
                  pl.BlockSpec(w_f.shape, fixed), pl.BlockSpec((1, LANES), fixed), pl.BlockSpec((ts, ts), fixed)],
        out_specs=[pl.BlockSpec((tm, width), lambda i, j: (i, j)),
                   pl.BlockSpec((tm, LANES), row), pl.BlockSpec((tm, LANES), row)],
        out_shape=[jax.ShapeDtypeStruct((m, 3 * width), BF16),
                   jax.ShapeDtypeStruct((m, LANES), BF16), jax.ShapeDtypeStruct((m, LANES), BF16)],
        scratch_shapes=[pltpu.VMEM((tm, k), BF16), pltpu.VMEM((8, LANES), F32)],
        compiler_params=_params("arbitrary", "arbitrary"),
        name="fox_in",
    )(x, w_t, w_f, b_f, tri)


def _attn_kernel(qm_ref, qa_ref, lm_ref, km_ref, ka_ref, v_ref, bias_ref, *rest, tq, hpb, n_cast):
    cast_in, (o_ref, *cast_out), (kcat, vcat, m_sc, acc_sc) = rest[:n_cast], rest[n_cast:2 * n_cast + 1], rest[2 * n_cast + 1:]
    i = pl.program_id(2)

    @pl.when(i == 0)
    def _():
        ones = jnp.ones(ka_ref.shape, BF16)
        for g in range(hpb):
            head = slice(g * LANES, (g + 1) * LANES)
            kcat[g, :, :LANES] = km_ref[:, head]
            kcat[g, :, LANES:] = ka_ref[...]
            vcat[g, :, :LANES] = v_ref[:, head]
            vcat[g, :, LANES:] = ones

    qa_tiles = qa_ref.shape[1] // LANES

    def q_tile(g):
        t = g * qa_tiles // hpb
        aux = qa_ref[:, t * LANES:(t + 1) * LANES] * lm_ref[g, 0:1, :]
        return jnp.concatenate([qm_ref[:, g * LANES:(g + 1) * LANES], aux], axis=1)

    q = [q_tile(g) for g in range(hpb)]

    nt = (((1,), (1,)), ((), ()))
    half = tq // 2

    def row_max(s):
        return jnp.broadcast_to(jnp.max(s, axis=-1, keepdims=True), (s.shape[0], LANES))

    def tile(g, j):
        rows = pl.ds(pl.multiple_of(j * tq, tq), tq)
        s = lax.dot_general(q[g], kcat[g, rows, :], nt, preferred_element_type=F32)
        m_new = jnp.maximum(m_sc[g], row_max(s))
        p = jnp.exp2(s - jnp.tile(m_new, (1, tq // LANES))).astype(BF16)
        pv = _dot(p, vcat[g, rows, :])
        acc_sc[g] = jnp.tile(jnp.exp2(m_sc[g] - m_new), (1, 2)) * acc_sc[g] + pv
        m_sc[g] = m_new

    def diagonal_tile(g, tri):
        left = pl.ds(pl.multiple_of(i * tq, tq), half)
        right = pl.ds(pl.multiple_of(i * tq + half, half), half)
        s_l = lax.dot_general(q[g], kcat[g, left, :], nt, preferred_element_type=F32)
        s_l = jnp.concatenate([s_l[:half] + tri, s_l[half:]], axis=0)
        s_r = lax.dot_general(q[g][half:], kcat[g, right, :], nt, preferred_element_type=F32) + tri
        m_top = row_max(s_l[:half])
        m_bot = jnp.maximum(row_max(s_l[half:]), row_max(s_r))
        m_new = jnp.concatenate([m_top, m_bot], axis=0)
        p_l = jnp.exp2(s_l - jnp.tile(m_new, (1, half // LANES))).astype(BF16)
        p_r = jnp.exp2(s_r - jnp.tile(m_bot, (1, half // LANES))).astype(BF16)
        pv_l = _dot(p_l, vcat[g, left, :])
        acc_sc[g, :half] = pv_l[:half]
        acc_sc[g, half:] = pv_l[half:] + _dot(p_r, vcat[g, right, :])
        m_sc[g] = m_new

    def tiles(js, with_diagonal=False):
        if with_diagonal:
            tri = bias_ref[...]
            for g in range(hpb):
                diagonal_tile(g, tri)
        for j in js:
            for g in range(hpb):
                tile(g, j)

    group = ATTN_TILES_PER_BLOCK
    rem = i % group
    for r in range(group):
        @pl.when(rem == r)
        def _(r=r):
            for src, dst in zip(cast_in, cast_out):
                dst[...] = src[...].astype(BF16)
            tiles([i - t for t in range(1, r + 1)], with_diagonal=True)

    def body(jj, carry):
        tiles([group * jj + t for t in range(group)])
        return carry

    lax.fori_loop(0, i // group, body, 0)
    for g in range(hpb):
        acc = acc_sc[g]
        o_ref[:, g * LANES:(g + 1) * LANES] = (acc[:, :LANES] / acc[:, LANES:]).astype(o_ref.dtype)


def _attention(qm, qm_col0, qa, qa_tiles, lane_mask, km, km_col0, ka, v, v_col0, granularity, batch, heads,
               cast_jobs=(), tq=512, hpb=ATTN_HEADS_PER_STEP):
    n = qm.shape[0]
    seq = n // batch
    nq = seq // tq
    wide = hpb * LANES
    head_blocks = heads // hpb
    steps = batch * head_blocks * nq
    q_row = lambda b, i: b * nq + i
    bias = _causal_bias(tq // 2, granularity)

    def cast_spec(arr):
        blocks = _cast_blocks(arr.shape[0], steps)
        return pl.BlockSpec((arr.shape[0] // blocks, arr.shape[1]),
                            lambda b, h, i: (jnp.minimum((b * head_blocks + h) * nq + i, blocks - 1), 0))

    cast_specs = [cast_spec(arr) for arr in cast_jobs]
    out, *copies = pl.pallas_call(
        functools.partial(_attn_kernel, tq=tq, hpb=hpb, n_cast=len(cast_jobs)),
        grid=(batch, head_blocks, nq),
        in_specs=[
            pl.BlockSpec((tq, wide), lambda b, h, i: (q_row(b, i), qm_col0 // hpb + h)),
            pl.BlockSpec((tq, max(qa_tiles, 1) * LANES), lambda b, h, i: (q_row(b, i), h if qa_tiles else 0)),
            pl.BlockSpec((hpb, 8, LANES), lambda b, h, i: (h, 0, 0)),
            pl.BlockSpec((seq, wide), lambda b, h, i: (b, km_col0 // hpb + h)),
            pl.BlockSpec((seq, LANES), lambda b, h, i: (b, 0)),
            pl.BlockSpec((seq, wide), lambda b, h, i: (b, v_col0 // hpb + h)),
            pl.BlockSpec(bias.shape, lambda b, h, i: (0, 0)),
        ] + cast_specs,
        out_specs=[pl.BlockSpec((tq, wide), lambda b, h, i: (q_row(b, i), h))] + cast_specs,
        out_shape=[jax.ShapeDtypeStruct((n, heads * LANES), BF16)]
        + [jax.ShapeDtypeStruct(arr.shape, BF16) for arr in cast_jobs],
        scratch_shapes=[pltpu.VMEM((hpb, seq, 2 * LANES), BF16), pltpu.VMEM((hpb, seq, 2 * LANES), BF16),
                        pltpu.VMEM((hpb, tq, LANES), F32), pltpu.VMEM((hpb, tq, 2 * LANES), F32)],
        compiler_params=_params("arbitrary", "arbitrary", "arbitrary"),
        name="attention",
    )(qm, qa, lane_mask, km, ka, v, bias, *cast_jobs)
    return out, copies


def _proj_ln_kernel(o_ref, w_ref, x_ref, g_ref, b_ref, y_ref, *, alpha):
    for r in range(0, x_ref.shape[0], PROJ_ROW_CHUNK):
        rows = slice(r, r + PROJ_ROW_CHUNK)
        y = alpha * x_ref[rows, :] + _dot(o_ref[rows, :], w_ref[...])
        y_ref[rows, :] = _layer_norm(y, g_ref[...], b_ref[...])


def _proj_ln(o, w, x, g, b, alpha, tm=512):
    n, d = x.shape
    row = lambda i: (i, 0)
    fixed = lambda i: (0, 0)
    return pl.pallas_call(
        functools.partial(_proj_ln_kernel, alpha=alpha),
        grid=(n // tm,),
        in_specs=[pl.BlockSpec((tm, o.shape[1]), row), pl.BlockSpec(w.shape, fixed),
                  pl.BlockSpec((tm, d), row), pl.BlockSpec((1, d), fixed), pl.BlockSpec((1, d), fixed)],
        out_specs=pl.BlockSpec((tm, d), row),
        out_shape=jax.ShapeDtypeStruct((n, d), F32),
        compiler_params=_params("arbitrary"),
        name="proj_ln",
    )(o, w, x, g, b)


def _ffn_kernel(x_ref, wg_ref, wu_ref, wd_ref, g_ref, b_ref, y_ref, x_bf, *, alpha):
    j = pl.program_id(1)

    @pl.when(j == 0)
    def _():
        x_bf[...] = x_ref[...].astype(BF16)
        y_ref[...] = jnp.zeros_like(y_ref)

    for r in range(0, x_bf.shape[0], FFN_ROW_CHUNK):
        rows = slice(r, r + FFN_ROW_CHUNK)
        xb = x_bf[rows, :]
        gate = _dot(xb, wg_ref[...])
        up = _dot(xb, wu_ref[...])
        y_ref[rows, :] += _dot((gate * jax.nn.sigmoid(gate) * up).astype(BF16), wd_ref[...])

    @pl.when(j == pl.num_programs(1) - 1)
    def _():
        for r in range(0, x_bf.shape[0], LN_ROW_CHUNK):
            rows = slice(r, r + LN_ROW_CHUNK)
            y_ref[rows, :] = _layer_norm(alpha * x_ref[rows, :] + y_ref[rows, :], g_ref[...], b_ref[...])


def _ffn(x, w_gu, w_down, layer, g, b, alpha, tm=1024, tf=512):
    n, d = x.shape
    hidden = w_down.shape[1]
    nf = hidden // tf
    return pl.pallas_call(
        functools.partial(_ffn_kernel, alpha=alpha),
        grid=(n // tm, nf),
        in_specs=[pl.BlockSpec((tm, d), lambda i, j: (i, 0)),
                  pl.BlockSpec((None, d, tf), lambda i, j: (layer, 0, j)),
                  pl.BlockSpec((None, d, tf), lambda i, j: (layer, 0, nf + j)),
                  pl.BlockSpec((None, tf, d), lambda i, j: (layer, j, 0)),
                  pl.BlockSpec((1, d), lambda i, j: (0, 0)), pl.BlockSpec((1, d), lambda i, j: (0, 0))],
        out_specs=pl.BlockSpec((tm, d), lambda i, j: (i, 0)),
        out_shape=jax.ShapeDtypeStruct((n, d), F32),
        scratch_shapes=[pltpu.VMEM((tm, d), BF16)],
        compiler_params=_params("arbitrary", "arbitrary"),
        name="ffn",
    )(x, w_gu, w_gu, w_down, g, b)


def _mla_weights(w_in, w_q_up, w_kv_up):
    d = w_in.shape[0]
    half = MLA_ROPE_DIM // 2
    base = MLA_Q_RANK + MLA_KV_RANK
    k1, k2 = w_in[:, base:base + half], w_in[:, base + half:]
    w_in_cat = jnp.concatenate([w_in[:, :base], k1, k2, k1, k2, -k2, k1, -k2, k1], axis=1)
    wq = w_q_up.reshape(MLA_Q_RANK, MLA_HEADS, MLA_NOPE_DIM + MLA_ROPE_DIM)
    r1 = wq[:, :, MLA_NOPE_DIM:MLA_NOPE_DIM + half]
    r2 = wq[:, :, MLA_NOPE_DIM + half:]
    w_q_cat = jnp.concatenate([
        wq[:, :, :MLA_NOPE_DIM].reshape(MLA_Q_RANK, -1),
        wq[:, :, MLA_NOPE_DIM:].reshape(MLA_Q_RANK, -1),
        jnp.concatenate([-r2, r1], axis=2).reshape(MLA_Q_RANK, -1)], axis=1)
    wkv = w_kv_up.reshape(MLA_KV_RANK, MLA_HEADS, MLA_NOPE_DIM + MLA_V_DIM)
    w_kv_cat = jnp.concatenate([wkv[:, :, :MLA_NOPE_DIM].reshape(MLA_KV_RANK, -1),
                                wkv[:, :, MLA_NOPE_DIM:].reshape(MLA_KV_RANK, -1)], axis=1)
    del d
    return w_in_cat.astype(BF16), w_q_cat.astype(BF16), w_kv_cat.astype(BF16)


def _cast_blocks(rows, steps):
    return max(b for b in range(1, steps + 1) if rows % (16 * b) == 0)


def _causal_bias(tq, granularity):
    r = np.arange(tq)[:, None] // granularity
    c = np.arange(tq)[None, :] // granularity
    return jnp.asarray(np.where(c <= r, 0.0, NEG_INF), F32)


def _mla_lane_mask():
    lane = np.arange(LANES)[None, :]
    head = np.arange(MLA_HEADS)[:, None]
    m = (lane // MLA_ROPE_DIM == head % 2).astype(np.float32)
    return jnp.asarray(np.broadcast_to(m[:, None, :], (MLA_HEADS, 8, LANES)), BF16)


def _fox_lane_mask():
    lane = np.arange(LANES)[None, :]
    head = np.arange(FOX_HEADS)[:, None]
    m = ((lane % FOX_HEADS == head) & (lane < 6 * FOX_HEADS)).astype(np.float32)
    return jnp.asarray(np.broadcast_to(m[:, None, :], (FOX_HEADS, 8, LANES)), BF16)


def _mla_layer(x, pos, w_in, q_norm_g, w_q_up, kv_norm_g, w_kv_up, batch, cast_jobs):
    w_in_cat, w_q_cat, w_kv_cat = _mla_weights(w_in, w_q_up, w_kv_up)
    half = MLA_ROPE_DIM // 2
    inv_freq = ROPE_THETA ** (-jnp.arange(0, MLA_ROPE_DIM, 2, dtype=F32) / MLA_ROPE_DIM)
    freq = jnp.tile(inv_freq, LANES // half)[None, :]
    scale = (MLA_NOPE_DIM + MLA_ROPE_DIM) ** -0.5 * LOG2E
    qn, qr, kv, kr = _mla_proj(x, pos, freq, w_in_cat, q_norm_g[None, :], kv_norm_g[None, :],
                               w_q_cat, w_kv_cat, scale)
    o, copies = _attention(qn, 0, qr, ATTN_HEADS_PER_STEP // 2, _mla_lane_mask(), kv, 0, kr, kv, MLA_HEADS,
                           CHUNK, batch, MLA_HEADS, cast_jobs)
    return o, copies


def _fox_layer(x, w_t, b_f, batch, cast_jobs):
    width = FOX_HEADS * FOX_HEAD_DIM
    groups = 6
    w_f = jnp.pad(jnp.tile(w_t[3 * width:].T, (1, groups)), ((0, 0), (0, LANES - groups * FOX_HEADS)))
    b6 = jnp.pad(jnp.tile(b_f, groups), (0, LANES - groups * FOX_HEADS))[None, :]
    qkv, aq, ak = _fox_in(x, w_t, w_f, b6, batch, FOX_HEAD_DIM ** -0.5 * LOG2E)
    o, copies = _attention(qkv, 0, aq, 0, _fox_lane_mask(), qkv, FOX_HEADS, ak, qkv, 2 * FOX_HEADS,
                           1, batch, FOX_HEADS, cast_jobs)
    return o, copies


def kernel(x, positions, mla_w_in, mla_q_norm_g, mla_w_q_up, mla_kv_norm_g, mla_w_kv_up, mla_w_o,
           fox_w_in, fox_b_f, fox_w_o, ffn_w_gu, ffn_w_down, ln_mix_g, ln_mix_b, ln_ffn_g, ln_ffn_b):
    batch, seq, d = x.shape
    depth = ffn_w_gu.shape[0]
    alpha = float((2 * depth) ** 0.25)
    h = x.reshape(batch * seq, d)
    pos = positions.astype(F32).reshape(batch * seq, 1)
    late = [ffn_w_gu, ffn_w_down, mla_w_o, fox_w_o, jnp.swapaxes(fox_w_in, 1, 2)]
    cast_jobs = [w.reshape(-1, w.shape[2]) for w in late]
    for i in range(depth):
        j = i // 2
        if i % 2 == 0:
            o, copies = _mla_layer(h, pos, mla_w_in[j], mla_q_norm_g[j], mla_w_q_up[j], mla_kv_norm_g[j],
                                   mla_w_kv_up[j], batch, cast_jobs)
            if cast_jobs:
                w_gu_bf, w_down_bf, mla_w_o_bf, fox_w_o_bf, fox_w_in_t_bf = [
                    c.reshape(w.shape) for c, w in zip(copies, late)]
                cast_jobs = []
            w_o = mla_w_o_bf[j]
        else:
            o, _ = _fox_layer(h, fox_w_in_t_bf[j], fox_b_f[j], batch, cast_jobs)
            w_o = fox_w_o_bf[j]
        h = _proj_ln(o, w_o, h, ln_mix_g[i][None, :], ln_mix_b[i][None, :], alpha)
        h = _ffn(h, w_gu_bf, w_down_bf, i, ln_ffn_g[i][None, :], ln_ffn_b[i][None, :], alpha)
    return h.reshape(batch, seq, d)
```

```python
import functools

import jax
import jax.numpy as jnp
import numpy as np
from jax import lax
from jax.experimental import pallas as pl
from jax.experimental.pallas import tpu as pltpu

CHUNK = 64
MLA_HEADS = 16
MLA_NOPE_DIM = 128
MLA_ROPE_DIM = 64
MLA_V_DIM = 128
MLA_Q_RANK = 512
MLA_KV_RANK = 512
ROPE_THETA = 10000.0
FOX_HEADS = 16
FOX_HEAD_DIM = 128
LN_EPS = 1e-5
RMS_EPS = 1e-6
NEG_INF = -1e30
LOG2E = 1.4426950408889634

LANES = 128
VMEM_LIMIT_BYTES = 62 * 1024 * 1024
ATTN_HEADS_PER_STEP = 4
ATTN_TILES_PER_BLOCK = 8
FFN_ROW_CHUNK = 512
LN_ROW_CHUNK = 256
MLA_KV_COL_CHUNK = 1024
PROJ_ROW_CHUNK = 128

BF16 = jnp.bfloat16
F32 = jnp.float32


def _params(*semantics):
    return pltpu.CompilerParams(dimension_semantics=semantics, vmem_limit_bytes=VMEM_LIMIT_BYTES)


def _dot(a, b):
    return jnp.dot(a, b, preferred_element_type=F32)


def _layer_norm(y, g, b):
    mu = jnp.mean(y, axis=-1, keepdims=True)
    d = y - mu
    var = jnp.mean(d * d, axis=-1, keepdims=True)
    return d * lax.rsqrt(var + LN_EPS) * g + b


def _rms_norm(y, g):
    return y * lax.rsqrt(jnp.mean(y * y, axis=-1, keepdims=True) + RMS_EPS) * g


def _mla_proj_kernel(x_ref, pos_ref, freq_ref, w_in_ref, gq_ref, gkv_ref, wq_ref, wkv_ref,
                     qn_ref, qr_ref, kv_ref, kr_ref, *, scale):
    n_nope = MLA_HEADS * MLA_NOPE_DIM
    n_rope = MLA_HEADS * MLA_ROPE_DIM
    base = MLA_Q_RANK + MLA_KV_RANK
    h = _dot(x_ref[...].astype(BF16), w_in_ref[...])
    cq = _rms_norm(h[:, :MLA_Q_RANK], gq_ref[...]).astype(BF16)
    ckv = _rms_norm(h[:, MLA_Q_RANK:base], gkv_ref[...]).astype(BF16)
    ang = pos_ref[...] * freq_ref[...]
    c = jnp.cos(ang)
    s = jnp.sin(ang)
    kr_ref[...] = (h[:, base:base + LANES] * c + h[:, base + LANES:] * s).astype(BF16)
    qn_ref[...] = (_dot(cq, wq_ref[:, :n_nope]) * scale).astype(BF16)
    r = _dot(cq, wq_ref[:, n_nope:n_nope + n_rope])
    rp = _dot(cq, wq_ref[:, n_nope + n_rope:])
    for p in range(n_rope // LANES):
        sl = slice(p * LANES, (p + 1) * LANES)
        qr_ref[:, sl] = ((r[:, sl] * c + rp[:, sl] * s) * scale).astype(BF16)
    for col in range(0, kv_ref.shape[1], MLA_KV_COL_CHUNK):
        cols = slice(col, col + MLA_KV_COL_CHUNK)
        kv_ref[:, cols] = _dot(ckv, wkv_ref[:, cols]).astype(BF16)


def _mla_proj(x, pos, freq, w_in, gq, gkv, wq, wkv, scale, tm=512):
    n, d = x.shape
    n_nope = MLA_HEADS * MLA_NOPE_DIM
    n_rope = MLA_HEADS * MLA_ROPE_DIM
    row = lambda i: (i, 0)
    fixed = lambda i: (0, 0)
    resident = lambda w: pl.BlockSpec(w.shape, fixed, pipeline_mode=pl.Buffered(1))
    return pl.pallas_call(
        functools.partial(_mla_proj_kernel, scale=scale),
        grid=(n // tm,),
        in_specs=[pl.BlockSpec((tm, d), row), pl.BlockSpec((tm, 1), row), pl.BlockSpec((1, LANES), fixed),
                  resident(w_in), pl.BlockSpec((1, MLA_Q_RANK), fixed), pl.BlockSpec((1, MLA_KV_RANK), fixed),
                  resident(wq), resident(wkv)],
        out_specs=[pl.BlockSpec((tm, n_nope), row), pl.BlockSpec((tm, n_rope), row),
                   pl.BlockSpec((tm, wkv.shape[1]), row), pl.BlockSpec((tm, LANES), row)],
        out_shape=[jax.ShapeDtypeStruct((n, n_nope), BF16), jax.ShapeDtypeStruct((n, n_rope), BF16),
                   jax.ShapeDtypeStruct((n, wkv.shape[1]), BF16), jax.ShapeDtypeStruct((n, LANES), BF16)],
        compiler_params=_params("arbitrary"),
        name="mla_proj",
    )(x, pos, freq, w_in, gq, gkv, wq, wkv)


def _split3(v):
    hi = v.astype(BF16).astype(F32)
    r1 = v - hi
    mid = r1.astype(BF16).astype(F32)
    lo = (r1 - mid).astype(BF16).astype(F32)
    return hi, mid, lo


def _fox_in_kernel(x_ref, wt_ref, wf_ref, bf_ref, tri_ref, o_ref, aq_ref, ak_ref, x_bf, carry,
                   *, scale, tiles_per_seq):
    i = pl.program_id(0)
    j = pl.program_id(1)

    @pl.when(j == 0)
    def _():
        x_bf[...] = x_ref[...].astype(BF16)

        @pl.when(i % tiles_per_seq == 0)
        def _():
            carry[...] = jnp.zeros_like(carry)

        logit = _dot(x_bf[...], wf_ref[...]) + bf_ref[...]
        log_f = jnp.minimum(logit, 0.0) - jnp.log1p(jnp.exp(-jnp.abs(logit)))
        tri = tri_ref[...]
        ts = tri.shape[0]
        group = lax.broadcasted_iota(jnp.int32, (ts, LANES), 1) // FOX_HEADS
        for r in range(0, x_bf.shape[0], ts):
            hi, mid, lo = _split3(log_f[r:r + ts])
            c = (_dot(tri, hi.astype(BF16)) + _dot(tri, mid.astype(BF16))) + _dot(tri, lo.astype(BF16))
            c = c + carry[0:1, :]
            carry[0:1, :] = c[ts - 1:ts, :]
            hi, mid, lo = _split3(c * LOG2E)
            pieces = jnp.where(group % 3 == 0, hi, jnp.where(group % 3 == 1, mid, lo))
            aq_ref[r:r + ts, :] = jnp.where(group < 3, pieces, jnp.where(group < 6, 1.0, 0.0)).astype(BF16)
            ak_ref[r:r + ts, :] = jnp.where(group < 3, 1.0, jnp.where(group < 6, -pieces, 0.0)).astype(BF16)

    acc = lax.dot_general(x_bf[...], wt_ref[...], (((1,), (1,)), ((), ())), preferred_element_type=F32)
    o_ref[...] = (acc * jnp.where(j == 0, scale, 1.0).astype(F32)).astype(o_ref.dtype)


def _fox_in(x, w_t, w_f, b_f, batch, scale, tm=1024, ts=512):
    m, k = x.shape
    width = FOX_HEADS * FOX_HEAD_DIM
    tri = jnp.asarray(np.tril(np.ones((ts, ts), np.float32)), BF16)
    row = lambda i, j: (i, 0)
    fixed = lambda i, j: (0, 0)
    return pl.pallas_call(
        functools.partial(_fox_in_kernel, scale=scale, tiles_per_seq=m // batch // tm),
        grid=(m // tm, 3),
        in_specs=[pl.BlockSpec((tm, k), row), pl.BlockSpec((width, k), lambda i, j: (j, 0)),
                  pl.BlockSpec(w_f.shape, fixed), pl.BlockSpec((1, LANES), fixed), pl.BlockSpec((ts, ts), fixed)],
        out_specs=[pl.BlockSpec((tm, width), lambda i, j: (i, j)),
                   pl.BlockSpec((tm, LANES), row), pl.BlockSpec((tm, LANES), row)],
        out_shape=[jax.ShapeDtypeStruct((m, 3 * width), BF16),
                   jax.ShapeDtypeStruct((m, LANES), BF16), jax.ShapeDtypeStruct((m, LANES), BF16)],
        scratch_shapes=[pltpu.VMEM((tm, k), BF16), pltpu.VMEM((8, LANES), F32)],
        compiler_params=_params("arbitrary", "arbitrary"),
        name="fox_in",
    )(x, w_t, w_f, b_f, tri)


def _attn_kernel(qm_ref, qa_ref, lm_ref, km_ref, ka_ref, v_ref, bias_ref, *rest, tq, hpb, n_cast, nq):
    cast_in, (o_ref, *cast_out), (kcat, vcat, m_sc, acc_sc) = rest[:n_cast], rest[n_cast:2 * n_cast + 1], rest[2 * n_cast + 1:]
    i = pl.program_id(2)

    @pl.when(i == 0)
    def _():
        ones = jnp.ones(ka_ref.shape, BF16)
        for g in range(hpb):
            head = slice(g * LANES, (g + 1) * LANES)
            kcat[g, :, :LANES] = km_ref[:, head]
            kcat[g, :, LANES:] = ka_ref[...]
            vcat[g, :, :LANES] = v_ref[:, head]
            vcat[g, :, LANES:] = ones

    qa_tiles = qa_ref.shape[1] // LANES

    def q_tile(g):
        t = g * qa_tiles // hpb
        aux = qa_ref[:, t * LANES:(t + 1) * LANES] * lm_ref[g, 0:1, :]
        return jnp.concatenate([qm_ref[:, g * LANES:(g + 1) * LANES], aux], axis=1)

    q = [q_tile(g) for g in range(hpb)]

    nt = (((1,), (1,)), ((), ()))
    half = tq // 2

    def row_max(s):
        return jnp.broadcast_to(jnp.max(s, axis=-1, keepdims=True), (s.shape[0], LANES))

    def tile(g, j):
        rows = pl.ds(pl.multiple_of(j * tq, tq), tq)
        s = lax.dot_general(q[g], kcat[g, rows, :], nt, preferred_element_type=F32)
        m_new = jnp.maximum(m_sc[g], row_max(s))
        p = jnp.exp2(s - jnp.tile(m_new, (1, tq // LANES))).astype(BF16)
        pv = _dot(p, vcat[g, rows, :])
        acc_sc[g] = jnp.tile(jnp.exp2(m_sc[g] - m_new), (1, 2)) * acc_sc[g] + pv
        m_sc[g] = m_new

    def diagonal_tile(g, tri):
        left = pl.ds(pl.multiple_of(i * tq, tq), half)
        right = pl.ds(pl.multiple_of(i * tq + half, half), half)
        s_l = lax.dot_general(q[g], kcat[g, left, :], nt, preferred_element_type=F32)
        s_l = jnp.concatenate([s_l[:half] + tri, s_l[half:]], axis=0)
        s_r = lax.dot_general(q[g][half:], kcat[g, right, :], nt, preferred_element_type=F32) + tri
        m_top = row_max(s_l[:half])
        m_bot = jnp.maximum(row_max(s_l[half:]), row_max(s_r))
        m_new = jnp.concatenate([m_top, m_bot], axis=0)
        p_l = jnp.exp2(s_l - jnp.tile(m_new, (1, half // LANES))).astype(BF16)
        p_r = jnp.exp2(s_r - jnp.tile(m_bot, (1, half // LANES))).astype(BF16)
        pv_l = _dot(p_l, vcat[g, left, :])
        acc_sc[g, :half] = pv_l[:half]
        acc_sc[g, half:] = pv_l[half:] + _dot(p_r, vcat[g, right, :])
        m_sc[g] = m_new

    def tiles(js, with_diagonal=False):
        if with_diagonal:
            tri = bias_ref[...]
            for g in range(hpb):
                diagonal_tile(g, tri)
        for j in js:
            for g in range(hpb):
                tile(g, j)

    group = ATTN_TILES_PER_BLOCK
    rem = i % group
    for r in range(group):
        @pl.when(rem == r)
        def _(r=r):
            for src, dst in zip(cast_in, cast_out):
                dst[...] = src[...].astype(BF16)
            tiles([i - t for t in range(1, r + 1)], with_diagonal=True)

    def body(jj, carry):
        tiles([group * jj + t for t in range(group)])
        return carry

    if nq > group:
        lax.fori_loop(0, i // group, body, 0)
    for g in range(hpb):
        acc = acc_sc[g]
        o_ref[:, g * LANES:(g + 1) * LANES] = (acc[:, :LANES] / acc[:, LANES:]).astype(o_ref.dtype)


def _attention(qm, qm_col0, qa, qa_tiles, lane_mask, km, km_col0, ka, v, v_col0, granularity, batch, heads,
               cast_jobs=(), tq=512, hpb=ATTN_HEADS_PER_STEP):
    n = qm.shape[0]
    seq = n // batch
    nq = seq // tq
    wide = hpb * LANES
    head_blocks = heads // hpb
    steps = batch * head_blocks * nq
    q_row = lambda b, i: b * nq + i
    bias = _causal_bias(tq // 2, granularity)

    def cast_spec(arr):
        blocks = _cast_blocks(arr.shape[0], steps)
        return pl.BlockSpec((arr.shape[0] // blocks, arr.shape[1]),
                            lambda b, h, i: (jnp.minimum((b * head_blocks + h) * nq + i, blocks - 1), 0))

    cast_specs = [cast_spec(arr) for arr in cast_jobs]
    out, *copies = pl.pallas_call(
        functools.partial(_attn_kernel, tq=tq, hpb=hpb, n_cast=len(cast_jobs), nq=nq),
        grid=(batch, head_blocks, nq),
        in_specs=[
            pl.BlockSpec((tq, wide), lambda b, h, i: (q_row(b, i), qm_col0 // hpb + h)),
            pl.BlockSpec((tq, max(qa_tiles, 1) * LANES), lambda b, h, i: (q_row(b, i), h if qa_tiles else 0)),
            pl.BlockSpec((hpb, 8, LANES), lambda b, h, i: (h, 0, 0)),
            pl.BlockSpec((seq, wide), lambda b, h, i: (b, km_col0 // hpb + h)),
            pl.BlockSpec((seq, LANES), lambda b, h, i: (b, 0)),
            pl.BlockSpec((seq, wide), lambda b, h, i: (b, v_col0 // hpb + h)),
            pl.BlockSpec(bias.shape, lambda b, h, i: (0, 0)),
        ] + cast_specs,
        out_specs=[pl.BlockSpec((tq, wide), lambda b, h, i: (q_row(b, i), h))] + cast_specs,
        out_shape=[jax.ShapeDtypeStruct((n, heads * LANES), BF16)]
        + [jax.ShapeDtypeStruct(arr.shape, BF16) for arr in cast_jobs],
        scratch_shapes=[pltpu.VMEM((hpb, seq, 2 * LANES), BF16), pltpu.VMEM((hpb, seq, 2 * LANES), BF16),
                        pltpu.VMEM((hpb, tq, LANES), F32), pltpu.VMEM((hpb, tq, 2 * LANES), F32)],
        compiler_params=_params("arbitrary", "arbitrary", "arbitrary"),
        name="attention",
    )(qm, qa, lane_mask, km, ka, v, bias, *cast_jobs)
    return out, copies


def _proj_ln_kernel(o_ref, w_ref, x_ref, g_ref, b_ref, y_ref, *, alpha):
    for r in range(0, x_ref.shape[0], PROJ_ROW_CHUNK):
        rows = slice(r, r + PROJ_ROW_CHUNK)
        y = alpha * x_ref[rows, :] + _dot(o_ref[rows, :], w_ref[...])
        y_ref[rows, :] = _layer_norm(y, g_ref[...], b_ref[...])


def _proj_ln(o, w, x, g, b, alpha, tm=512):
    n, d = x.shape
    row = lambda i: (i, 0)
    fixed = lambda i: (0, 0)
    return pl.pallas_call(
        functools.partial(_proj_ln_kernel, alpha=alpha),
        grid=(n // tm,),
        in_specs=[pl.BlockSpec((tm, o.shape[1]), row), pl.BlockSpec(w.shape, fixed),
                  pl.BlockSpec((tm, d), row), pl.BlockSpec((1, d), fixed), pl.BlockSpec((1, d), fixed)],
        out_specs=pl.BlockSpec((tm, d), row),
        out_shape=jax.ShapeDtypeStruct((n, d), F32),
        compiler_params=_params("arbitrary"),
        name="proj_ln",
    )(o, w, x, g, b)


def _ffn_kernel(x_ref, wg_ref, wu_ref, wd_ref, g_ref, b_ref, y_ref, x_bf, *, alpha):
    j = pl.program_id(1)

    @pl.when(j == 0)
    def _():
        x_bf[...] = x_ref[...].astype(BF16)
        y_ref[...] = jnp.zeros_like(y_ref)

    for r in range(0, x_bf.shape[0], FFN_ROW_CHUNK):
        rows = slice(r, r + FFN_ROW_CHUNK)
        xb = x_bf[rows, :]
        gate = _dot(xb, wg_ref[...])
        up = _dot(xb, wu_ref[...])
        y_ref[rows, :] += _dot((gate * jax.nn.sigmoid(gate) * up).astype(BF16), wd_ref[...])

    @pl.when(j == pl.num_programs(1) - 1)
    def _():
        for r in range(0, x_bf.shape[0], LN_ROW_CHUNK):
            rows = slice(r, r + LN_ROW_CHUNK)
            y_ref[rows, :] = _layer_norm(alpha * x_ref[rows, :] + y_ref[rows, :], g_ref[...], b_ref[...])


def _ffn(x, w_gu, w_down, layer, g, b, alpha, tm=1024, tf=512):
    n, d = x.shape
    hidden = w_down.shape[1]
    nf = hidden // tf
    return pl.pallas_call(
        functools.partial(_ffn_kernel, alpha=alpha),
        grid=(n // tm, nf),
        in_specs=[pl.BlockSpec((tm, d), lambda i, j: (i, 0)),
                  pl.BlockSpec((None, d, tf), lambda i, j: (layer, 0, j)),
                  pl.BlockSpec((None, d, tf), lambda i, j: (layer, 0, nf + j)),
                  pl.BlockSpec((None, tf, d), lambda i, j: (layer, j, 0)),
                  pl.BlockSpec((1, d), lambda i, j: (0, 0)), pl.BlockSpec((1, d), lambda i, j: (0, 0))],
        out_specs=pl.BlockSpec((tm, d), lambda i, j: (i, 0)),
        out_shape=jax.ShapeDtypeStruct((n, d), F32),
        scratch_shapes=[pltpu.VMEM((tm, d), BF16)],
        compiler_params=_params("arbitrary", "arbitrary"),
        name="ffn",
    )(x, w_gu, w_gu, w_down, g, b)


def _mla_weights(w_in, w_q_up, w_kv_up):
    d = w_in.shape[0]
    half = MLA_ROPE_DIM // 2
    base = MLA_Q_RANK + MLA_KV_RANK
    k1, k2 = w_in[:, base:base + half], w_in[:, base + half:]
    w_in_cat = jnp.concatenate([w_in[:, :base], k1, k2, k1, k2, -k2, k1, -k2, k1], axis=1)
    wq = w_q_up.reshape(MLA_Q_RANK, MLA_HEADS, MLA_NOPE_DIM + MLA_ROPE_DIM)
    r1 = wq[:, :, MLA_NOPE_DIM:MLA_NOPE_DIM + half]
    r2 = wq[:, :, MLA_NOPE_DIM + half:]
    w_q_cat = jnp.concatenate([
        wq[:, :, :MLA_NOPE_DIM].reshape(MLA_Q_RANK, -1),
        wq[:, :, MLA_NOPE_DIM:].reshape(MLA_Q_RANK, -1),
        jnp.concatenate([-r2, r1], axis=2).reshape(MLA_Q_RANK, -1)], axis=1)
    wkv = w_kv_up.reshape(MLA_KV_RANK, MLA_HEADS, MLA_NOPE_DIM + MLA_V_DIM)
    w_kv_cat = jnp.concatenate([wkv[:, :, :MLA_NOPE_DIM].reshape(MLA_KV_RANK, -1),
                                wkv[:, :, MLA_NOPE_DIM:].reshape(MLA_KV_RANK, -1)], axis=1)
    del d
    return w_in_cat.astype(BF16), w_q_cat.astype(BF16), w_kv_cat.astype(BF16)


def _cast_blocks(rows, steps):
    return max(b for b in range(1, steps + 1) if rows % (16 * b) == 0)


def _causal_bias(tq, granularity):
    r = np.arange(tq)[:, None] // granularity
    c = np.arange(tq)[None, :] // granularity
    return jnp.asarray(np.where(c <= r, 0.0, NEG_INF), F32)


def _mla_lane_mask():
    lane = np.arange(LANES)[None, :]
    head = np.arange(MLA_HEADS)[:, None]
    m = (lane // MLA_ROPE_DIM == head % 2).astype(np.float32)
    return jnp.asarray(np.broadcast_to(m[:, None, :], (MLA_HEADS, 8, LANES)), BF16)


def _fox_lane_mask():
    lane = np.arange(LANES)[None, :]
    head = np.arange(FOX_HEADS)[:, None]
    m = ((lane % FOX_HEADS == head) & (lane < 6 * FOX_HEADS)).astype(np.float32)
    return jnp.asarray(np.broadcast_to(m[:, None, :], (FOX_HEADS, 8, LANES)), BF16)


def _mla_layer(x, pos, w_in, q_norm_g, w_q_up, kv_norm_g, w_kv_up, batch, cast_jobs):
    w_in_cat, w_q_cat, w_kv_cat = _mla_weights(w_in, w_q_up, w_kv_up)
    half = MLA_ROPE_DIM // 2
    inv_freq = ROPE_THETA ** (-jnp.arange(0, MLA_ROPE_DIM, 2, dtype=F32) / MLA_ROPE_DIM)
    freq = jnp.tile(inv_freq, LANES // half)[None, :]
    scale = (MLA_NOPE_DIM + MLA_ROPE_DIM) ** -0.5 * LOG2E
    qn, qr, kv, kr = _mla_proj(x, pos, freq, w_in_cat, q_norm_g[None, :], kv_norm_g[None, :],
                               w_q_cat, w_kv_cat, scale)
    o, copies = _attention(qn, 0, qr, ATTN_HEADS_PER_STEP // 2, _mla_lane_mask(), kv, 0, kr, kv, MLA_HEADS,
                           CHUNK, batch, MLA_HEADS, cast_jobs)
    return o, copies


def _fox_layer(x, w_t, b_f, batch, cast_jobs):
    width = FOX_HEADS * FOX_HEAD_DIM
    groups = 6
    w_f = jnp.pad(jnp.tile(w_t[3 * width:].T, (1, groups)), ((0, 0), (0, LANES - groups * FOX_HEADS)))
    b6 = jnp.pad(jnp.tile(b_f, groups), (0, LANES - groups * FOX_HEADS))[None, :]
    qkv, aq, ak = _fox_in(x, w_t, w_f, b6, batch, FOX_HEAD_DIM ** -0.5 * LOG2E)
    o, copies = _attention(qkv, 0, aq, 0, _fox_lane_mask(), qkv, FOX_HEADS, ak, qkv, 2 * FOX_HEADS,
                           1, batch, FOX_HEADS, cast_jobs)
    return o, copies


def kernel(x, positions, mla_w_in, mla_q_norm_g, mla_w_q_up, mla_kv_norm_g, mla_w_kv_up, mla_w_o,
           fox_w_in, fox_b_f, fox_w_o, ffn_w_gu, ffn_w_down, ln_mix_g, ln_mix_b, ln_ffn_g, ln_ffn_b):
    batch, seq, d = x.shape
    depth = ffn_w_gu.shape[0]
    alpha = float((2 * depth) ** 0.25)
    h = x.reshape(batch * seq, d)
    pos = positions.astype(F32).reshape(batch * seq, 1)
    late = [ffn_w_gu, ffn_w_down, mla_w_o, fox_w_o, jnp.swapaxes(fox_w_in, 1, 2)]
    cast_jobs = [w.reshape(-1, w.shape[2]) for w in late]
    for i in range(depth):
        j = i // 2
        if i % 2 == 0:
            o, copies = _mla_layer(h, pos, mla_w_in[j], mla_q_norm_g[j], mla_w_q_up[j], mla_kv_norm_g[j],
                                   mla_w_kv_up[j], batch, cast_jobs)
            if cast_jobs:
                w_gu_bf, w_down_bf, mla_w_o_bf, fox_w_o_bf, fox_w_in_t_bf = [
                    c.reshape(w.shape) for c, w in zip(copies, late)]
                cast_jobs = []
            w_o = mla_w_o_bf[j]
        else:
            o, _ = _fox_layer(h, fox_w_in_t_bf[j], fox_b_f[j], batch, cast_jobs)
            w_o = fox_w_o_bf[j]
        h = _proj_ln(o, w_o, h, ln_mix_g[i][None, :], ln_mix_b[i][None, :], alpha)
        h = _ffn(h, w_gu_bf, w_down_bf, i, ln_ffn_g[i][None, :], ln_ffn_b[i][None, :], alpha)
    return h.reshape(batch, seq, d)
```
